```python
import math
import jax, jax.numpy as jnp
from jax import lax
import numpy as np


D_MODEL = 1024
BATCH = 8
SEQ = 4096
DEPTH = 1

HEAD_DIM = 64
SWA_Q_HEADS = 8
SWA_KV_HEADS = 2
SWA_GROUP = SWA_Q_HEADS // SWA_KV_HEADS
SWA_WINDOW = 128
FOX_HEADS = 8
BLOCK = 128
ROPE_THETA = 10000.0
PEER_HEADS = 8
PEER_NKEYS = 128
PEER_EXPERTS = PEER_NKEYS * PEER_NKEYS
PEER_QDIM = 256
PEER_HALF = PEER_QDIM // 2
PEER_TOPK = 16
PEER_CHUNK = 128
N_MOD = 6
EPS = 1e-6
NEG_INF = -1e30

SWA_Q_W = SWA_Q_HEADS * HEAD_DIM
SWA_KV_W = SWA_KV_HEADS * HEAD_DIM
FOX_W = FOX_HEADS * HEAD_DIM
IN_SPLITS = (SWA_Q_W, SWA_KV_W, SWA_KV_W, FOX_W, FOX_W, FOX_W, FOX_HEADS, D_MODEL, D_MODEL)
IN_COLS = SWA_Q_W + 2 * SWA_KV_W + 3 * FOX_W + FOX_HEADS + 2 * D_MODEL

kernel_name = 'hybrid_swa_fox_peer_block'


def rmsnorm(x, g):
    xf = x.astype(jnp.float32)
    y = xf * lax.rsqrt(jnp.mean(xf * xf, axis=-1, keepdims=True) + EPS)
    return (y * g.astype(jnp.float32)).astype(x.dtype)


def apply_rope(x, positions):
    half = HEAD_DIM // 2
    inv_freq = ROPE_THETA ** (-jnp.arange(half, dtype=jnp.float32) / half)
    ang = positions.astype(jnp.float32)[:, :, None] * inv_freq
    cos = jnp.cos(ang)[:, :, None, :]
    sin = jnp.sin(ang)[:, :, None, :]
    xf = x.astype(jnp.float32)
    x1, x2 = xf[..., :half], xf[..., half:]
    return jnp.concatenate([x1 * cos - x2 * sin, x2 * cos + x1 * sin], axis=-1).astype(x.dtype)


def sliding_window_attention(q, k, v, sinks):
    B, S = q.shape[0], q.shape[1]
    nb = S // BLOCK
    qb = q.reshape(B, nb, BLOCK, SWA_KV_HEADS, SWA_GROUP, HEAD_DIM)
    kb = k.reshape(B, nb, BLOCK, SWA_KV_HEADS, HEAD_DIM)
    vb = v.reshape(B, nb, BLOCK, SWA_KV_HEADS, HEAD_DIM)
    pad = ((0, 0), (1, 0), (0, 0), (0, 0), (0, 0))
    k_band = jnp.concatenate([jnp.pad(kb, pad)[:, :-1], kb], axis=2)
    v_band = jnp.concatenate([jnp.pad(vb, pad)[:, :-1], vb], axis=2)
    s = jnp.einsum('bnqhgd,bnkhd->bnhgqk', qb, k_band).astype(jnp.float32) * (HEAD_DIM ** -0.5)
    q_loc = jnp.arange(BLOCK)[:, None] + BLOCK
    k_loc = jnp.arange(2 * BLOCK)[None, :]
    rel = q_loc - k_loc
    in_window = (rel >= 0) & (rel < SWA_WINDOW)
    key_pos = jnp.arange(nb)[:, None] * BLOCK - BLOCK + jnp.arange(2 * BLOCK)[None, :]
    mask = in_window[None] & (key_pos >= 0)[:, None, :]
    s = jnp.where(mask[None, :, None, None], s, NEG_INF)
    sink = sinks.astype(jnp.float32).reshape(SWA_KV_HEADS, SWA_GROUP)[None, None, :, :, None, None]
    sink = jnp.broadcast_to(sink, s.shape[:-1] + (1,))
    p = jax.nn.softmax(jnp.concatenate([s, sink], axis=-1), axis=-1)[..., :-1]
    o = jnp.einsum('bnhgqk,bnkhd->bnqhgd', p.astype(v.dtype), v_band)
    return o.reshape(B, S, SWA_Q_HEADS * HEAD_DIM)


def forgetting_attention(q, k, v, log_f):
    B, S = q.shape[0], q.shape[1]
    nb = S // BLOCK
    dcum = jnp.cumsum(log_f, axis=1)
    dcum_keys = dcum.transpose(0, 2, 1)
    q_blocks = q.reshape(B, nb, BLOCK, FOX_HEADS, HEAD_DIM).transpose(1, 0, 2, 3, 4)
    d_blocks = dcum.reshape(B, nb, BLOCK, FOX_HEADS).transpose(1, 0, 3, 2)
    key_pos = jnp.arange(S)
    scale = HEAD_DIM ** -0.5

    def one_block(args):
        q_blk, d_blk, i = args
        s = jnp.einsum('bqhd,bkhd->bhqk', q_blk, k).astype(jnp.float32) * scale
        s = s + d_blk[..., :, None] - dcum_keys[..., None, :]
        q_pos = i * BLOCK + jnp.arange(BLOCK)
        s = jnp.where(key_pos[None, :] <= q_pos[:, None], s, NEG_INF)
        p = jax.nn.softmax(s, axis=-1)
        return jnp.einsum('bhqk,bkhd->bqhd', p.astype(v.dtype), v)

    o = lax.map(one_block, (q_blocks, d_blocks, jnp.arange(nb)))
    return o.transpose(1, 0, 2, 3, 4).reshape(B, S, FOX_HEADS * HEAD_DIM)


def peer(h, w_query, sub_keys, expert_u, expert_v):
    B, S, D = h.shape
    h_chunks = h.reshape((B * S) // PEER_CHUNK, PEER_CHUNK, D)

    def chunk(hc):
        q = (hc @ w_query).reshape(PEER_CHUNK, PEER_HEADS, 2, PEER_HALF)
        s = jnp.einsum('thpc,hpnc->thpn', q, sub_keys).astype(jnp.float32)
        sv, si = lax.top_k(s, PEER_TOPK)
        cand = sv[:, :, 0, :, None] + sv[:, :, 1, None, :]
        cidx = si[:, :, 0, :, None] * PEER_NKEYS + si[:, :, 1, None, :]
        cand = cand.reshape(PEER_CHUNK, PEER_HEADS, PEER_TOPK * PEER_TOPK)
        cidx = cidx.reshape(PEER_CHUNK, PEER_HEADS, PEER_TOPK * PEER_TOPK)
        top_s, pos = lax.top_k(cand, PEER_TOPK)
        eidx = jnp.take_along_axis(cidx, pos, axis=-1)
        g = jax.nn.softmax(top_s, axis=-1)
        u = expert_u[eidx]
        a = jax.nn.gelu(jnp.einsum('thkd,td->thk', u, hc).astype(jnp.float32), approximate=False)
        vv = expert_v[eidx]
        return jnp.einsum('thk,thkd->td', (g * a).astype(hc.dtype), vv)

    return lax.map(chunk, h_chunks).reshape(B, S, D)


def setup_inputs(seed: int = 0) -> dict:
    key = jax.random.key(seed)
    ks = jax.random.split(key, 24)
    f32 = jnp.float32
    L, D = DEPTH, D_MODEL

    def nrm(k, shape, scale):
        return jax.random.normal(k, shape, f32) * scale

    x = nrm(ks[0], (BATCH, SEQ, D), 1.0)
    c = nrm(ks[1], (BATCH, D), 1.0)
    offsets = jax.random.randint(ks[2], (BATCH, 1), 0, 1024, dtype=jnp.int32)
    positions = offsets + jnp.arange(SEQ, dtype=jnp.int32)[None, :]
    return {
        'x': x,
        'c': c,
        'positions': positions,
        'w_mod': nrm(ks[3], (L, D, N_MOD * D), 0.5 * D ** -0.5),
        'b_mod': nrm(ks[4], (L, N_MOD * D), 0.02),
        'norm1_g': 1.0 + nrm(ks[5], (L, D), 0.02),
        'w_in': nrm(ks[6], (L, D, IN_COLS), D ** -0.5),
        'q_norm_swa': 1.0 + nrm(ks[7], (L, HEAD_DIM), 0.02),
        'k_norm_swa': 1.0 + nrm(ks[8], (L, HEAD_DIM), 0.02),
        'sinks': nrm(ks[9], (L, SWA_Q_HEADS), 1.0),
        'q_norm_fox': 1.0 + nrm(ks[10], (L, HEAD_DIM), 0.02),
        'k_norm_fox': 1.0 + nrm(ks[11], (L, HEAD_DIM), 0.02),
        'b_forget': 3.0 + nrm(ks[12], (L, FOX_HEADS), 0.5),
        'w_out_swa': nrm(ks[13], (L, SWA_Q_W, D), SWA_Q_W ** -0.5),
        'w_out_fox': nrm(ks[14], (L, FOX_W, D), FOX_W ** -0.5),
        'w_o': nrm(ks[15], (L, D, D), D ** -0.5),
        'norm2_g': 1.0 + nrm(ks[16], (L, D), 0.02),
        'peer_w_query': nrm(ks[17], (L, D, PEER_HEADS * PEER_QDIM), D ** -0.5),
        'peer_sub_keys': nrm(ks[18], (L, PEER_HEADS, 2, PEER_NKEYS, PEER_HALF), PEER_HALF ** -0.5),
        'peer_u': nrm(ks[19], (L, PEER_EXPERTS, D), D ** -0.5),
        'peer_v': nrm(ks[20], (L, PEER_EXPERTS, D), 0.5 * PEER_HEADS ** -0.5),
    }


def reference(x, c, positions, w_mod, b_mod, norm1_g, w_in, q_norm_swa, k_norm_swa, sinks,
              q_norm_fox, k_norm_fox, b_forget, w_out_swa, w_out_fox, w_o, norm2_g,
              peer_w_query, peer_sub_keys, peer_u, peer_v):
    B, S, D = x.shape
    split_at = [int(v) for v in np.cumsum(IN_SPLITS)[:-1]]
    for l in range(DEPTH):
        mod = jax.nn.silu(c) @ w_mod[l] + b_mod[l]
        shift1, scale1, gate1, shift2, scale2, gate2 = jnp.split(mod, N_MOD, axis=-1)

        h = rmsnorm(x, norm1_g[l]) * (1.0 + scale1[:, None, :]) + shift1[:, None, :]
        proj = h @ w_in[l]
        qa, ka, va, qf, kf, vf, f_logit, g_a, g_b = jnp.split(proj, split_at, axis=-1)

        qa = apply_rope(rmsnorm(qa.reshape(B, S, SWA_Q_HEADS, HEAD_DIM), q_norm_swa[l]), positions)
        ka = apply_rope(rmsnorm(ka.reshape(B, S, SWA_KV_HEADS, HEAD_DIM), k_norm_swa[l]), positions)
        va = va.reshape(B, S, SWA_KV_HEADS, HEAD_DIM)
        out_a = sliding_window_attention(qa, ka, va, sinks[l])

        qf = rmsnorm(qf.reshape(B, S, FOX_HEADS, HEAD_DIM), q_norm_fox[l])
        kf = rmsnorm(kf.reshape(B, S, FOX_HEADS, HEAD_DIM), k_norm_fox[l])
        vf = vf.reshape(B, S, FOX_HEADS, HEAD_DIM)
        log_f = jax.nn.log_sigmoid(f_logit.astype(jnp.float32) + b_forget[l].astype(jnp.float32))
        out_b = forgetting_attention(qf, kf, vf, log_f)

        merged = jax.nn.sigmoid(g_a) * (out_a @ w_out_swa[l]) + jax.nn.sigmoid(g_b) * (out_b @ w_out_fox[l])
        x = x + gate1[:, None, :] * (merged @ w_o[l])

        h2 = rmsnorm(x, norm2_g[l]) * (1.0 + scale2[:, None, :]) + shift2[:, None, :]
        x = x + gate2[:, None, :] * peer(h2, peer_w_query[l], peer_sub_keys[l], peer_u[l], peer_v[l])
    return x
```

```python
import functools

import jax
import jax.numpy as jnp
import numpy as np
from jax import lax
from jax.experimental import pallas as pl
from jax.experimental.pallas import tpu as pltpu

D_MODEL = 1024
HEAD_DIM = 64
HEAD_PAD = 128
SWA_Q_HEADS = 8
SWA_KV_HEADS = 2
SWA_GROUP = SWA_Q_HEADS // SWA_KV_HEADS
SWA_WINDOW = 128
FOX_HEADS = 8
BLOCK = 128
ROPE_THETA = 10000.0
PEER_HEADS = 8
PEER_NKEYS = 128
PEER_EXPERTS = PEER_NKEYS * PEER_NKEYS
PEER_QDIM = 256
PEER_HALF = PEER_QDIM // 2
PEER_TOPK = 16
N_MOD = 6
EPS = 1e-6
NEG_INF = -1e30

SWA_Q_W = SWA_Q_HEADS * HEAD_DIM
SWA_KV_W = SWA_KV_HEADS * HEAD_DIM
FOX_W = FOX_HEADS * HEAD_DIM
IN_SPLITS = (SWA_Q_W, SWA_KV_W, SWA_KV_W, FOX_W, FOX_W, FOX_W, FOX_HEADS, D_MODEL, D_MODEL)

C_QA = 0
C_KA = C_QA + SWA_Q_HEADS * HEAD_PAD
C_VA = C_KA + SWA_KV_HEADS * HEAD_PAD
C_QF = C_VA + SWA_KV_W
C_KF = C_QF + FOX_HEADS * HEAD_PAD
C_VF = C_KF + FOX_HEADS * HEAD_PAD
C_F = C_VF + FOX_W
C_END = C_F + HEAD_PAD

TM_PRE = 512
TQ_SWA = 512
TQ_FOX = 512
TT_ROUTE = 256
TT_PEER = 512
TE_PEER = 512
VMEM_LIMIT = 56 * 1024 * 1024

BF16 = jnp.bfloat16
F32 = jnp.float32


def _split3(v):
    hi = v.astype(BF16)
    r1 = v - hi.astype(F32)
    mid = r1.astype(BF16)
    lo = (r1 - mid.astype(F32)).astype(BF16)
    return hi, mid, lo


def _dot_nt(a, b):
    return lax.dot_general(a, b, (((1,), (1,)), ((), ())), preferred_element_type=F32)


def _dot(a, b):
    return jnp.dot(a, b, preferred_element_type=F32)


def _rms_modulate(x, g, scale, shift):
    y = x * lax.rsqrt(jnp.mean(x * x, axis=-1, keepdims=True) + EPS)
    return (y * g) * (1.0 + scale) + shift


def _mod_kernel(c_ref, w_ref, b_ref, o_ref):
    c = c_ref[...]
    a = c * jax.nn.sigmoid(c)
    a_hi, a_mid, a_lo = _split3(a)
    w = w_ref[...]
    w_hi, w_mid, w_lo = _split3(w)
    acc = _dot(a_hi, w_hi) + (_dot(a_hi, w_mid) + _dot(a_mid, w_hi))
    acc = acc + (_dot(a_mid, w_mid) + _dot(a_hi, w_lo) + _dot(a_lo, w_hi))
    o_ref[...] = acc + b_ref[...]


def _mod_call(c, w_mod, b_mod):
    bsz = c.shape[0]
    n = w_mod.shape[1]
    tn = 1024
    return pl.pallas_call(
        _mod_kernel,
        out_shape=jax.ShapeDtypeStruct((bsz, n), F32),
        grid=(n // tn,),
        in_specs=[
            pl.BlockSpec((bsz, D_MODEL), lambda j: (0, 0)),
            pl.BlockSpec((D_MODEL, tn), lambda j: (0, j)),
            pl.BlockSpec((1, tn), lambda j: (0, j)),
        ],
        out_specs=pl.BlockSpec((bsz, tn), lambda j: (0, j)),
        compiler_params=pltpu.CompilerParams(
            dimension_semantics=("parallel",), vmem_limit_bytes=VMEM_LIMIT),
        name="mod",
    )(c, w_mod, b_mod.reshape(1, n))


def _pre_kernel(x_ref, mod_ref, g1_ref, w_ref, gqa_ref, gka_ref, gqf_ref, gkf_ref, bf_ref,
                pos_ref, invf_ref, sign_ref, tri_ref, selq_ref, selk_ref, oneq_ref, onek_ref,
                qa_ref, ka_ref, va_ref, qf_ref, kf_ref, vf_ref, carry_ref):
    s_idx = pl.program_id(1)

    @pl.when(s_idx == 0)
    def _():
        carry_ref[...] = jnp.zeros_like(carry_ref)

    x = x_ref[0]
    mod = mod_ref[0]
    h = _rms_modulate(x, g1_ref[...], mod[1:2, :], mod[0:1, :]).astype(BF16)

    ang = pos_ref[0].astype(F32) * invf_ref[...]
    cos = jnp.cos(ang)
    sin_s = jnp.sin(ang) * sign_ref[...]
    lane = lax.broadcasted_iota(jnp.int32, (1, HEAD_PAD), 1)
    low_half = lane < HEAD_DIM // 2

    def head_norm(c, g):
        return c * lax.rsqrt(jnp.sum(c * c, axis=-1, keepdims=True) * (1.0 / HEAD_DIM) + EPS) * g

    def rope(c):
        partner = jnp.where(low_half, pltpu.roll(c, HEAD_PAD - HEAD_DIM // 2, 1),
                            pltpu.roll(c, HEAD_DIM // 2, 1))
        return c * cos + partner * sin_s

    scale = HEAD_DIM ** -0.5

    for hh in range(SWA_Q_HEADS):
        c = _dot(h, w_ref[:, C_QA + hh * HEAD_PAD:C_QA + (hh + 1) * HEAD_PAD])
        c = rope(head_norm(c, gqa_ref[:, hh * HEAD_PAD:(hh + 1) * HEAD_PAD])) * scale
        qa_ref[0, :, hh * HEAD_PAD:(hh + 1) * HEAD_PAD] = c.astype(BF16)
    for hh in range(SWA_KV_HEADS):
        c = _dot(h, w_ref[:, C_KA + hh * HEAD_PAD:C_KA + (hh + 1) * HEAD_PAD])
        c = rope(head_norm(c, gka_ref[:, hh * HEAD_PAD:(hh + 1) * HEAD_PAD]))
        ka_ref[0, :, hh * HEAD_PAD:(hh + 1) * HEAD_PAD] = c.astype(BF16)
    va_ref[0] = _dot(h, w_ref[:, C_VA:C_VA + SWA_KV_W]).astype(BF16)
    vf_ref[0] = _dot(h, w_ref[:, C_VF:C_VF + FOX_W]).astype(BF16)

    f_logit = _dot(h, w_ref[:, C_F:C_F + HEAD_PAD]) + bf_ref[...]
    log_f = jnp.minimum(f_logit, 0.0) - jnp.log1p(jnp.exp(-jnp.abs(f_logit)))
    l_hi, l_mid, l_lo = _split3(log_f)
    tri = tri_ref[...]
    dcum = (_dot(tri, l_hi) + _dot(tri, l_mid) + _dot(tri, l_lo)) + carry_ref[...]
    tm = dcum.shape[0]
    carry_ref[...] = dcum[tm - 1:tm, :]

    d_hi, d_mid, d_lo = _split3(dcum)
    d3 = jnp.concatenate([d_hi, d_mid, d_lo], axis=1)
    aug_q = _dot(d3, selq_ref[...]) + oneq_ref[...]
    aug_k = onek_ref[...] - _dot(d3, selk_ref[...])

    for hh in range(FOX_HEADS):
        sl = slice(hh * HEAD_PAD, (hh + 1) * HEAD_PAD)
        c = _dot(h, w_ref[:, C_QF + hh * HEAD_PAD:C_QF + (hh + 1) * HEAD_PAD])
        c = head_norm(c, gqf_ref[:, sl]) * scale + aug_q[:, sl]
        qf_ref[0, :, sl] = c.astype(BF16)
        c = _dot(h, w_ref[:, C_KF + hh * HEAD_PAD:C_KF + (hh + 1) * HEAD_PAD])
        c = head_norm(c, gkf_ref[:, sl]) + aug_k[:, sl]
        kf_ref[0, :, sl] = c.astype(BF16)


def _pad_heads_cols(w, nheads):
    k = w.shape[0]
    w = w.reshape(k, nheads, HEAD_DIM)
    w = jnp.pad(w, ((0, 0), (0, 0), (0, HEAD_PAD - HEAD_DIM)))
    return w.reshape(k, nheads * HEAD_PAD)


def _pad_head_gain(g, nheads):
    g = jnp.pad(g.astype(F32), (0, HEAD_PAD - HEAD_DIM))
    return jnp.tile(g, nheads).reshape(1, nheads * HEAD_PAD)


def _decay_selectors():
    selq = np.zeros((3 * HEAD_PAD, FOX_HEADS * HEAD_PAD), np.float32)
    selk = np.zeros((3 * HEAD_PAD, FOX_HEADS * HEAD_PAD), np.float32)
    oneq = np.zeros((1, FOX_HEADS * HEAD_PAD), np.float32)
    onek = np.zeros((1, FOX_HEADS * HEAD_PAD), np.float32)
    for hh in range(FOX_HEADS):
        for p in range(3):
            selq[p * HEAD_PAD + hh, hh * HEAD_PAD + HEAD_DIM + p] = 1.0
            selk[p * HEAD_PAD + hh, hh * HEAD_PAD + HEAD_DIM + 3 + p] = 1.0
            oneq[0, hh * HEAD_PAD + HEAD_DIM + 3 + p] = 1.0
            onek[0, hh * HEAD_PAD + HEAD_DIM + p] = 1.0
    return (jnp.asarray(selq, BF16), jnp.asarray(selk, BF16), jnp.asarray(oneq), jnp.asarray(onek))


def _pre_call(x, mod3, g1, w1, gqa, gka, gqf, gkf, bf_pad, positions):
    bsz, seq, _ = x.shape
    tm = TM_PRE
    half = HEAD_DIM // 2
    inv_freq = ROPE_THETA ** (-jnp.arange(half, dtype=F32) / half)
    invf = jnp.concatenate([inv_freq, inv_freq, jnp.zeros((HEAD_PAD - HEAD_DIM,), F32)]).reshape(1, HEAD_PAD)
    sign = jnp.concatenate([-jnp.ones((half,), F32), jnp.ones((half,), F32),
                            jnp.zeros((HEAD_PAD - HEAD_DIM,), F32)]).reshape(1, HEAD_PAD)
    tri = jnp.asarray(np.tril(np.ones((tm, tm), np.float32)), BF16)
    selq, selk, oneq, onek = _decay_selectors()

    def full(a):
        return pl.BlockSpec(a.shape, lambda b, s: (0,) * a.ndim)

    def tok(width):
        return pl.BlockSpec((1, tm, width), lambda b, s: (b, s, 0))

    consts = (g1, w1, gqa, gka, gqf, gkf, bf_pad)
    tail = (invf, sign, tri, selq, selk, oneq, onek)
    out_w = (SWA_Q_HEADS * HEAD_PAD, SWA_KV_HEADS * HEAD_PAD, SWA_KV_W,
             FOX_HEADS * HEAD_PAD, FOX_HEADS * HEAD_PAD, FOX_W)
    return pl.pallas_call(
        _pre_kernel,
        out_shape=[jax.ShapeDtypeStruct((bsz, seq, w), BF16) for w in out_w],
        grid=(bsz, seq // tm),
        in_specs=[tok(D_MODEL), pl.BlockSpec((1, N_MOD, D_MODEL), lambda b, s: (b, 0, 0))]
        + [full(a) for a in consts] + [tok(1)] + [full(a) for a in tail],
        out_specs=[tok(w) for w in out_w],
        scratch_shapes=[pltpu.VMEM((1, HEAD_PAD), F32)],
        compiler_params=pltpu.CompilerParams(
            dimension_semantics=("parallel", "arbitrary"), vmem_limit_bytes=VMEM_LIMIT),
        name="pre",
    )(x, mod3, *consts, positions.reshape(bsz, seq, 1), *tail)


def _swa_kernel(sink_ref, q_ref, kc_ref, kp_ref, vc_ref, vp_ref, o_ref):
    t_idx = pl.program_id(1)
    tq = q_ref.shape[1]
    nsub = tq // BLOCK
    k_all = jnp.concatenate([kp_ref[0], kc_ref[0]], axis=0)
    v_all = jnp.concatenate([vp_ref[0], vc_ref[0]], axis=0)
    v_swap = jnp.concatenate([v_all[:, HEAD_DIM:], v_all[:, :HEAD_DIM]], axis=1)

    rows = SWA_GROUP * BLOCK
    r = lax.broadcasted_iota(jnp.int32, (rows, 2 * BLOCK), 0)
    kcol = lax.broadcasted_iota(jnp.int32, (rows, 2 * BLOCK), 1)
    rel = (r % BLOCK) + BLOCK - kcol
    in_window = (rel >= 0) & (rel < SWA_WINDOW)
    rgrp = lax.broadcasted_iota(jnp.int32, (rows, 1), 0) // BLOCK
    lane_lo = lax.broadcasted_iota(jnp.int32, (1, HEAD_PAD), 1) < HEAD_DIM

    for j in range(nsub):
        lo_key = jnp.where(t_idx == 0, BLOCK, 0) if j == 0 else 0
        mask = in_window & (kcol >= lo_key)
        vb = v_all[j * BLOCK:(j + 2) * BLOCK, :]
        vsb = v_swap[j * BLOCK:(j + 2) * BLOCK, :]
        for g in range(SWA_KV_HEADS):
            kb = k_all[j * BLOCK:(j + 2) * BLOCK, g * HEAD_PAD:(g + 1) * HEAD_PAD]
            qg = jnp.concatenate(
                [q_ref[0, j * BLOCK:(j + 1) * BLOCK,
                       (g * SWA_GROUP + u) * HEAD_PAD:(g * SWA_GROUP + u + 1) * HEAD_PAD]
                 for u in range(SWA_GROUP)], axis=0)
            s = jnp.where(mask, _dot_nt(qg, kb), NEG_INF)
            sink = jnp.zeros((rows, 1), F32)
            for u in range(SWA_GROUP):
                sink = jnp.where(rgrp == u, sink_ref[g * SWA_GROUP + u], sink)
            m = jnp.maximum(jnp.max(s, axis=-1, keepdims=True), sink)
            p = jnp.exp(s - m)
            denom = jnp.sum(p, axis=-1, keepdims=True) + jnp.exp(sink - m)
            pb = p.astype(BF16)
            inv = 1.0 / denom
            o_nat = _dot(pb, vb) * inv
            o_swp = _dot(pb, vsb) * inv
            o_lo, o_hi = (o_nat, o_swp) if g == 0 else (o_swp, o_nat)
            for w in range(SWA_GROUP // 2):
                ev = o_lo[(2 * w) * BLOCK:(2 * w + 1) * BLOCK, :]
                od = o_hi[(2 * w + 1) * BLOCK:(2 * w + 2) * BLOCK, :]
                pc = g * (SWA_GROUP // 2) + w
                o_ref[0, j * BLOCK:(j + 1) * BLOCK, pc * HEAD_PAD:(pc + 1) * HEAD_PAD] = (
                    jnp.where(lane_lo, ev, od).astype(BF16))


def _swa_call(qa, ka, va, sinks):
    bsz, seq, _ = qa.shape
    tq = TQ_SWA
    per = tq // BLOCK

    def cur(width):
        return pl.BlockSpec((1, tq, width), lambda b, t: (b, t, 0))

    def prev(width):
        return pl.BlockSpec((1, BLOCK, width), lambda b, t: (b, jnp.maximum(t * per - 1, 0), 0))

    return pl.pallas_call(
        _swa_kernel,
        out_shape=jax.ShapeDtypeStruct((bsz, seq, SWA_Q_W), BF16),
        grid=(bsz, seq // tq),
        in_specs=[pl.BlockSpec(memory_space=pltpu.SMEM),
                  cur(SWA_Q_HEADS * HEAD_PAD),
                  cur(SWA_KV_HEADS * HEAD_PAD), prev(SWA_KV_HEADS * HEAD_PAD),
                  cur(SWA_KV_W), prev(SWA_KV_W)],
        out_specs=cur(SWA_Q_W),
        compiler_params=pltpu.CompilerParams(
            dimension_semantics=("parallel", "parallel"), vmem_limit_bytes=VMEM_LIMIT),
        name="swa",
    )(sinks.astype(F32), qa, ka, ka, va, va)


def _fox_kernel(q_ref, k_ref, v_ref, o_ref):
    qi = pl.program_id(2)
    tq = q_ref.shape[1]
    tk = tq
    lane_lo = lax.broadcasted_iota(jnp.int32, (1, HEAD_PAD), 1) < HEAD_DIM
    row = lax.broadcasted_iota(jnp.int32, (tq, tk), 0)
    col = lax.broadcasted_iota(jnp.int32, (tq, tk), 1)
    causal = col <= row
    outs = []
    for e in range(2):
        q = q_ref[0, :, e * HEAD_PAD:(e + 1) * HEAD_PAD]

        def step(j, carry, masked):
            m, l, acc = carry
            start = pl.multiple_of(j * tk, tk)
            k = k_ref[0, pl.ds(start, tk), e * HEAD_PAD:(e + 1) * HEAD_PAD]
            v = v_ref[0, pl.ds(start, tk), :]
            s = _dot_nt(q, k)
            if masked:
                s = jnp.where(causal, s, NEG_INF)
            m_new = jnp.maximum(m, jnp.max(s, axis=-1, keepdims=True))
            alpha = jnp.exp(m - m_new)
            p = jnp.exp(s - m_new)
            l = alpha * l + jnp.sum(p, axis=-1, keepdims=True)
            acc = alpha * acc + _dot(p.astype(BF16), v)
            return m_new, l, acc

        init = (jnp.full((tq, 1), NEG_INF, F32), jnp.zeros((tq, 1), F32),
                jnp.zeros((tq, HEAD_PAD), F32))
        carry = lax.fori_loop(0, qi, functools.partial(step, masked=False), init)
        m, l, acc = step(qi, carry, True)
        outs.append(acc * (1.0 / l))
    o_ref[0] = jnp.where(lane_lo, outs[0], outs[1]).astype(BF16)


def _fox_call(qf, kf, vf):
    bsz, seq, _ = qf.shape
    tq = TQ_FOX
    pairs = FOX_HEADS // 2
    return pl.pallas_call(
        _fox_kernel,
        out_shape=jax.ShapeDtypeStruct((bsz, seq, FOX_W), BF16),
        grid=(bsz, pairs, seq // tq),
        in_specs=[pl.BlockSpec((1, tq, 2 * HEAD_PAD), lambda b, p, t: (b, t, p)),
                  pl.BlockSpec((1, seq, 2 * HEAD_PAD), lambda b, p, t: (b, 0, p)),
                  pl.BlockSpec((1, seq, 2 * HEAD_DIM), lambda b, p, t: (b, 0, p))],
        out_specs=pl.BlockSpec((1, tq, 2 * HEAD_DIM), lambda b, p, t: (b, t, p)),
        compiler_params=pltpu.CompilerParams(
            dimension_semantics=("parallel", "parallel", "arbitrary"),
            vmem_limit_bytes=VMEM_LIMIT),
        name="fox",
    )(qf, kf, vf)


def _post_kernel(x_ref, mod_ref, g1_ref, wg_ref, oa_ref, ob_ref, wa_ref, wb_ref, wo_ref, g2_ref,
                 x1_ref, h2_ref):
    x = x_ref[0]
    mod = mod_ref[0]
    h = _rms_modulate(x, g1_ref[...], mod[1:2, :], mod[0:1, :]).astype(BF16)
    gate_a = jax.nn.sigmoid(_dot(h, wg_ref[:, :D_MODEL]))
    gate_b = jax.nn.sigmoid(_dot(h, wg_ref[:, D_MODEL:]))
    merged = gate_a * _dot(oa_ref[0], wa_ref[...]) + gate_b * _dot(ob_ref[0], wb_ref[...])
    x1 = x + mod[2:3, :] * _dot(merged.astype(BF16), wo_ref[...])
    x1_ref[0] = x1
    h2_ref[0] = _rms_modulate(x1, g2_ref[...], mod[4:5, :], mod[3:4, :]).astype(BF16)


def _post_call(x, mod3, g1, wg, out_a, out_b, wa, wb, wo, g2):
    bsz, seq, _ = x.shape
    tm = TM_PRE

    def full(a):
        return pl.BlockSpec(a.shape, lambda b, s: (0,) * a.ndim)

    def tok(width):
        return pl.BlockSpec((1, tm, width), lambda b, s: (b, s, 0))

    return pl.pallas_call(
        _post_kernel,
        out_shape=[jax.ShapeDtypeStruct((bsz, seq, D_MODEL), F32),
                   jax.ShapeDtypeStruct((bsz, seq, D_MODEL), BF16)],
        grid=(bsz, seq // tm),
        in_specs=[tok(D_MODEL), pl.BlockSpec((1, N_MOD, D_MODEL), lambda b, s: (b, 0, 0)),
                  full(g1), full(wg), tok(SWA_Q_W), tok(FOX_W), full(wa), full(wb), full(wo),
                  full(g2)],
        out_specs=[tok(D_MODEL), tok(D_MODEL)],
        compiler_params=pltpu.CompilerParams(
            dimension_semantics=("parallel", "parallel"), vmem_limit_bytes=VMEM_LIMIT),
        name="post",
    )(x, mod3, g1, wg, out_a, out_b, wa, wb, wo, g2)


def _top16(s):
    cur = s
    rank = jnp.full(s.shape, float(PEER_TOPK), F32)
    vals = []
    for r in range(PEER_TOPK):
        mx = jnp.max(cur, axis=0, keepdims=True)
        vals.append(mx)
        hit = cur == mx
        rank = jnp.where(hit, float(r), rank)
        cur = jnp.where(hit, -3.0e38, cur)
    return vals, rank


def _route_kernel(h_ref, wq_ref, keys_ref, n_ref, c0_ref, r1_ref, e1_ref, qt_ref):
    qt_ref[...] = _dot_nt(wq_ref[...], h_ref[...]).astype(BF16)

    def head(hh, _):
        base = pl.multiple_of(hh * PEER_QDIM, PEER_QDIM)
        q0 = qt_ref[pl.ds(base, PEER_HALF), :]
        q1 = qt_ref[pl.ds(base + PEER_HALF, PEER_HALF), :]
        s0 = _dot(keys_ref[2 * hh], q0)
        s1 = _dot(keys_ref[2 * hh + 1], q1)
        v0, rank0 = _top16(s0)
        v1, rank1 = _top16(s1)
        v1_all = jnp.concatenate(v1, axis=0)
        cand = jnp.concatenate([v0[a] + v1_all for a in range(PEER_TOPK)], axis=0)
        cur = cand
        tau = None
        for r in range(PEER_TOPK):
            tau = jnp.max(cur, axis=0, keepdims=True)
            cur = jnp.where(cur == tau, -3.0e38, cur)
        top = v0[0] + v1[0]
        sel = jnp.where(cand >= tau, 1.0, 0.0)
        z = jnp.sum(sel * jnp.exp(cand - top), axis=0, keepdims=True)
        cnt = jnp.zeros_like(s0)
        for a in range(PEER_TOPK):
            n_a = jnp.sum(sel[a * PEER_TOPK:(a + 1) * PEER_TOPK, :], axis=0, keepdims=True)
            cnt = jnp.where(rank0 == float(a), n_a, cnt)
        rows = pl.ds(pl.multiple_of(hh * PEER_NKEYS, PEER_NKEYS), PEER_NKEYS)
        n_ref[rows, :] = cnt
        c0_ref[rows, :] = jnp.exp(s0 - v0[0]) * (1.0 / z)
        r1_ref[rows, :] = rank1
        e1_ref[rows, :] = jnp.exp(s1 - v1[0])
        return 0

    lax.fori_loop(0, PEER_HEADS, head, 0)


def _route_call(h2, wq_t, keys):
    t_all = h2.shape[0]
    tt = TT_ROUTE
    rows = PEER_HEADS * PEER_NKEYS
    out = jax.ShapeDtypeStruct((rows, t_all), F32)
    return pl.pallas_call(
        _route_kernel,
        out_shape=[out, out, out, out],
        grid=(t_all // tt,),
        in_specs=[pl.BlockSpec((tt, D_MODEL), lambda t: (t, 0)),
                  pl.BlockSpec(wq_t.shape, lambda t: (0, 0)),
                  pl.BlockSpec(keys.shape, lambda t: (0, 0, 0))],
        out_specs=[pl.BlockSpec((rows, tt), lambda t: (0, t))] * 4,
        scratch_shapes=[pltpu.VMEM((PEER_HEADS * PEER_QDIM, tt), BF16)],
        compiler_params=pltpu.CompilerParams(
            dimension_semantics=("parallel",), vmem_limit_bytes=VMEM_LIMIT),
        name="route",
    )(h2, wq_t, keys)


def _peer_kernel(h_ref, u_ref, vt_ref, n_ref, c0_ref, r1_ref, e1_ref, x1_ref, mod_ref,
                 o_ref, acc_ref, w_ref):
    e_idx = pl.program_id(1)
    te = u_ref.shape[0]
    blocks = te // PEER_NKEYS

    @pl.when(e_idx == 0)
    def _():
        acc_ref[...] = jnp.zeros_like(acc_ref)

    a = _dot_nt(u_ref[...], h_ref[...])
    gelu = 0.5 * a * (1.0 + lax.erf(a * (2.0 ** -0.5)))
    for ii in range(blocks):
        i = e_idx * blocks + ii
        p = None
        for hh in range(PEER_HEADS):
            n_row = n_ref[pl.ds(hh * PEER_NKEYS + i, 1), :]
            c_row = c0_ref[pl.ds(hh * PEER_NKEYS + i, 1), :]
            r1 = r1_ref[hh * PEER_NKEYS:(hh + 1) * PEER_NKEYS, :]
            e1 = e1_ref[hh * PEER_NKEYS:(hh + 1) * PEER_NKEYS, :]
            term = jnp.where(r1 < n_row, e1 * c_row, 0.0)
            p = term if p is None else p + term
        w_ref[ii * PEER_NKEYS:(ii + 1) * PEER_NKEYS, :] = (
            p * gelu[ii * PEER_NKEYS:(ii + 1) * PEER_NKEYS, :]).astype(BF16)
    acc_ref[...] += _dot(vt_ref[...], w_ref[...])

    @pl.when(e_idx == pl.num_programs(1) - 1)
    def _():
        o_ref[...] = x1_ref[...] + mod_ref[0][5:6, :] * acc_ref[...].T


def _peer_call(h2, u_b, vt_b, n_t, c0_t, r1_t, e1_t, x1, mod3, tiles_per_batch):
    t_all = h2.shape[0]
    tt, te = TT_PEER, TE_PEER
    rows = PEER_HEADS * PEER_NKEYS

    def tab():
        return pl.BlockSpec((rows, tt), lambda t, e: (0, t))

    return pl.pallas_call(
        _peer_kernel,
        out_shape=jax.ShapeDtypeStruct((t_all, D_MODEL), F32),
        grid=(t_all // tt, PEER_EXPERTS // te),
        in_specs=[pl.BlockSpec((tt, D_MODEL), lambda t, e: (t, 0)),
                  pl.BlockSpec((te, D_MODEL), lambda t, e: (e, 0)),
                  pl.BlockSpec((D_MODEL, te), lambda t, e: (0, e)),
                  tab(), tab(), tab(), tab(),
                  pl.BlockSpec((tt, D_MODEL), lambda t, e: (t, 0)),
                  pl.BlockSpec((1, N_MOD, D_MODEL), lambda t, e: (t // tiles_per_batch, 0, 0))],
        out_specs=pl.BlockSpec((tt, D_MODEL), lambda t, e: (t, 0)),
        scratch_shapes=[pltpu.VMEM((D_MODEL, tt), F32), pltpu.VMEM((te, tt), BF16)],
        compiler_params=pltpu.CompilerParams(
            dimension_semantics=("parallel", "arbitrary"), vmem_limit_bytes=VMEM_LIMIT),
        name="peer",
    )(h2, u_b, vt_b, n_t, c0_t, r1_t, e1_t, x1, mod3)


def _layer(x, c, positions, w_mod, b_mod, norm1_g, w_in, q_norm_swa, k_norm_swa, sinks,
           q_norm_fox, k_norm_fox, b_forget, w_out_swa, w_out_fox, w_o, norm2_g,
           peer_w_query, peer_sub_keys, peer_u, peer_v):
    bsz, seq, _ = x.shape
    split_at = [int(v) for v in np.cumsum(IN_SPLITS)[:-1]]
    w_qa, w_ka, w_va, w_qf, w_kf, w_vf, w_f, w_ga, w_gb = jnp.split(w_in, split_at, axis=1)
    w1 = jnp.concatenate(
        [_pad_heads_cols(w_qa, SWA_Q_HEADS), _pad_heads_cols(w_ka, SWA_KV_HEADS), w_va,
         _pad_heads_cols(w_qf, FOX_HEADS), _pad_heads_cols(w_kf, FOX_HEADS), w_vf,
         jnp.pad(w_f, ((0, 0), (0, HEAD_PAD - FOX_HEADS)))], axis=1).astype(BF16)
    wg = jnp.concatenate([w_ga, w_gb], axis=1).astype(BF16)
    bf_pad = jnp.pad(b_forget.astype(F32), (0, HEAD_PAD - FOX_HEADS)).reshape(1, HEAD_PAD)
    g1 = norm1_g.astype(F32).reshape(1, D_MODEL)
    g2 = norm2_g.astype(F32).reshape(1, D_MODEL)

    mod3 = _mod_call(c, w_mod, b_mod).reshape(bsz, N_MOD, D_MODEL)

    qa, ka, va, qf, kf, vf = _pre_call(
        x, mod3, g1, w1,
        _pad_head_gain(q_norm_swa, SWA_Q_HEADS), _pad_head_gain(k_norm_swa, SWA_KV_HEADS),
        _pad_head_gain(q_norm_fox, FOX_HEADS), _pad_head_gain(k_norm_fox, FOX_HEADS),
        bf_pad, positions)

    out_a = _swa_call(qa, ka, va, sinks)
    out_b = _fox_call(qf, kf, vf)

    x1, h2 = _post_call(x, mod3, g1, wg, out_a, out_b, w_out_swa.astype(BF16),
                        w_out_fox.astype(BF16), w_o.astype(BF16), g2)

    t_all = bsz * seq
    h2f = h2.reshape(t_all, D_MODEL)
    keys = peer_sub_keys.reshape(PEER_HEADS * 2, PEER_NKEYS, PEER_HALF).astype(BF16)
    n_t, c0_t, r1_t, e1_t = _route_call(h2f, peer_w_query.T.astype(BF16), keys)
    out = _peer_call(h2f, peer_u.astype(BF16), peer_v.T.astype(BF16), n_t, c0_t, r1_t, e1_t,
                     x1.reshape(t_all, D_MODEL), mod3, seq // TT_PEER)
    return out.reshape(bsz, seq, D_MODEL)


def kernel(x, c, positions, w_mod, b_mod, norm1_g, w_in, q_norm_swa, k_norm_swa, sinks, q_norm_fox, k_norm_fox, b_forget, w_out_swa, w_out_fox, w_o, norm2_g, peer_w_query, peer_sub_keys, peer_u, peer_v):
    for l in range(w_mod.shape[0]):
        x = _layer(x, c, positions, w_mod[l], b_mod[l], norm1_g[l], w_in[l], q_norm_swa[l],
                   k_norm_swa[l], sinks[l], q_norm_fox[l], k_norm_fox[l], b_forget[l],
                   w_out_swa[l], w_out_fox[l], w_o[l], norm2_g[l], peer_w_query[l],
                   peer_sub_keys[l], peer_u[l], peer_v[l])
    return x
```

```python
import functools

import jax
import jax.numpy as jnp
import numpy as np
from jax import lax
from jax.experimental import pallas as pl
from jax.experimental.pallas import tpu as pltpu

D_MODEL = 1024
HEAD_DIM = 64
HEAD_PAD = 128
SWA_Q_HEADS = 8
SWA_KV_HEADS = 2
SWA_GROUP = SWA_Q_HEADS // SWA_KV_HEADS
SWA_WINDOW = 128
FOX_HEADS = 8
BLOCK = 128
ROPE_THETA = 10000.0
PEER_HEADS = 8
PEER_NKEYS = 128
PEER_EXPERTS = PEER_NKEYS * PEER_NKEYS
PEER_QDIM = 256
PEER_HALF = PEER_QDIM // 2
PEER_TOPK = 16
N_MOD = 6
EPS = 1e-6
NEG_INF = -1e30

SWA_Q_W = SWA_Q_HEADS * HEAD_DIM
SWA_KV_W = SWA_KV_HEADS * HEAD_DIM
FOX_W = FOX_HEADS * HEAD_DIM
IN_SPLITS = (SWA_Q_W, SWA_KV_W, SWA_KV_W, FOX_W, FOX_W, FOX_W, FOX_HEADS, D_MODEL, D_MODEL)

C_QA = 0
C_KA = C_QA + SWA_Q_HEADS * HEAD_PAD
C_VA = C_KA + SWA_KV_HEADS * HEAD_PAD
C_QF = C_VA + SWA_KV_W
C_KF = C_QF + FOX_HEADS * HEAD_PAD
C_VF = C_KF + FOX_HEADS * HEAD_PAD
C_F = C_VF + FOX_W
C_END = C_F + HEAD_PAD

TM_PRE = 512
TQ_SWA = 512
TQ_FOX = 512
TT_ROUTE = 256
TT_PEER = 512
TE_PEER = 1024
VMEM_LIMIT = 56 * 1024 * 1024

BF16 = jnp.bfloat16
F32 = jnp.float32
BF16_ROWS = 16


def _split3(v):
    hi = v.astype(BF16)
    r1 = v - hi.astype(F32)
    mid = r1.astype(BF16)
    lo = (r1 - mid.astype(F32)).astype(BF16)
    return hi, mid, lo


def _dot_nt(a, b):
    return lax.dot_general(a, b, (((1,), (1,)), ((), ())), preferred_element_type=F32)


def _dot(a, b):
    return jnp.dot(a, b, preferred_element_type=F32)


def _rms_modulate(x, g, scale, shift):
    y = x * lax.rsqrt(jnp.mean(x * x, axis=-1, keepdims=True) + EPS)
    return (y * g) * (1.0 + scale) + shift


def _mod_kernel(c_ref, w_ref, b_ref, o_ref):
    c = c_ref[...]
    a = c * jax.nn.sigmoid(c)
    a_hi, a_mid, a_lo = _split3(a)
    w = w_ref[...]
    w_hi, w_mid, w_lo = _split3(w)
    acc = _dot(a_hi, w_hi) + (_dot(a_hi, w_mid) + _dot(a_mid, w_hi))
    acc = acc + (_dot(a_mid, w_mid) + _dot(a_hi, w_lo) + _dot(a_lo, w_hi))
    o_ref[...] = acc + b_ref[...]


def _mod_call(c, w_mod, b_mod):
    bsz = c.shape[0]
    n = w_mod.shape[1]
    tn = 1024
    return pl.pallas_call(
        _mod_kernel,
        out_shape=jax.ShapeDtypeStruct((bsz, n), F32),
        grid=(n // tn,),
        in_specs=[
            pl.BlockSpec((bsz, D_MODEL), lambda j: (0, 0)),
            pl.BlockSpec((D_MODEL, tn), lambda j: (0, j)),
            pl.BlockSpec((1, tn), lambda j: (0, j)),
        ],
        out_specs=pl.BlockSpec((bsz, tn), lambda j: (0, j)),
        compiler_params=pltpu.CompilerParams(
            dimension_semantics=("parallel",), vmem_limit_bytes=VMEM_LIMIT),
        name="mod",
    )(c, w_mod, b_mod.reshape(1, n))


def _pre_kernel(x_ref, mod_ref, g1_ref, w_ref, gqa_ref, gka_ref, gqf_ref, gkf_ref, bf_ref,
                pos_ref, invf_ref, sign_ref, tri_ref, selq_ref, selk_ref, oneq_ref, onek_ref,
                qa_ref, ka_ref, va_ref, qf_ref, kf_ref, vf_ref, carry_ref):
    s_idx = pl.program_id(1)

    @pl.when(s_idx == 0)
    def _():
        carry_ref[...] = jnp.zeros_like(carry_ref)

    x = x_ref[0]
    mod = mod_ref[0]
    h = _rms_modulate(x, g1_ref[...], mod[1:2, :], mod[0:1, :]).astype(BF16)

    ang = pos_ref[0].astype(F32) * invf_ref[...]
    cos = jnp.cos(ang)
    sin_s = jnp.sin(ang) * sign_ref[...]
    lane = lax.broadcasted_iota(jnp.int32, (1, HEAD_PAD), 1)
    low_half = lane < HEAD_DIM // 2

    def head_norm(c, g):
        return c * lax.rsqrt(jnp.sum(c * c, axis=-1, keepdims=True) * (1.0 / HEAD_DIM) + EPS) * g

    def rope(c):
        partner = jnp.where(low_half, pltpu.roll(c, HEAD_PAD - HEAD_DIM // 2, 1),
                            pltpu.roll(c, HEAD_DIM // 2, 1))
        return c * cos + partner * sin_s

    scale = HEAD_DIM ** -0.5

    for hh in range(SWA_Q_HEADS):
        c = _dot(h, w_ref[:, C_QA + hh * HEAD_PAD:C_QA + (hh + 1) * HEAD_PAD])
        c = rope(head_norm(c, gqa_ref[:, hh * HEAD_PAD:(hh + 1) * HEAD_PAD])) * scale
        qa_ref[0, :, hh * HEAD_PAD:(hh + 1) * HEAD_PAD] = c.astype(BF16)
    for hh in range(SWA_KV_HEADS):
        c = _dot(h, w_ref[:, C_KA + hh * HEAD_PAD:C_KA + (hh + 1) * HEAD_PAD])
        c = rope(head_norm(c, gka_ref[:, hh * HEAD_PAD:(hh + 1) * HEAD_PAD]))
        ka_ref[0, :, hh * HEAD_PAD:(hh + 1) * HEAD_PAD] = c.astype(BF16)
    va_ref[0] = _dot(h, w_ref[:, C_VA:C_VA + SWA_KV_W]).astype(BF16)
    vf_ref[0] = _dot(h, w_ref[:, C_VF:C_VF + FOX_W]).astype(BF16)

    f_logit = _dot(h, w_ref[:, C_F:C_F + HEAD_PAD]) + bf_ref[...]
    log_f = jnp.minimum(f_logit, 0.0) - jnp.log1p(jnp.exp(-jnp.abs(f_logit)))
    l_hi, l_mid, l_lo = _split3(log_f)
    tri = tri_ref[...]
    dcum = (_dot(tri, l_hi) + _dot(tri, l_mid) + _dot(tri, l_lo)) + carry_ref[...]
    tm = dcum.shape[0]
    carry_ref[...] = dcum[tm - 1:tm, :]

    d_hi, d_mid, d_lo = _split3(dcum)
    d3 = jnp.concatenate([d_hi, d_mid, d_lo], axis=1)
    aug_q = _dot(d3, selq_ref[...]) + oneq_ref[...]
    aug_k = onek_ref[...] - _dot(d3, selk_ref[...])

    for hh in range(FOX_HEADS):
        sl = slice(hh * HEAD_PAD, (hh + 1) * HEAD_PAD)
        c = _dot(h, w_ref[:, C_QF + hh * HEAD_PAD:C_QF + (hh + 1) * HEAD_PAD])
        c = head_norm(c, gqf_ref[:, sl]) * scale + aug_q[:, sl]
        qf_ref[0, :, sl] = c.astype(BF16)
        c = _dot(h, w_ref[:, C_KF + hh * HEAD_PAD:C_KF + (hh + 1) * HEAD_PAD])
        c = head_norm(c, gkf_ref[:, sl]) + aug_k[:, sl]
        kf_ref[0, :, sl] = c.astype(BF16)


def _pad_heads_cols(w, nheads):
    k = w.shape[0]
    w = w.reshape(k, nheads, HEAD_DIM)
    w = jnp.pad(w, ((0, 0), (0, 0), (0, HEAD_PAD - HEAD_DIM)))
    return w.reshape(k, nheads * HEAD_PAD)


def _pad_head_gain(g, nheads):
    g = jnp.pad(g.astype(F32), (0, HEAD_PAD - HEAD_DIM))
    return jnp.tile(g, nheads).reshape(1, nheads * HEAD_PAD)


def _decay_selectors():
    selq = np.zeros((3 * HEAD_PAD, FOX_HEADS * HEAD_PAD), np.float32)
    selk = np.zeros((3 * HEAD_PAD, FOX_HEADS * HEAD_PAD), np.float32)
    oneq = np.zeros((1, FOX_HEADS * HEAD_PAD), np.float32)
    onek = np.zeros((1, FOX_HEADS * HEAD_PAD), np.float32)
    for hh in range(FOX_HEADS):
        for p in range(3):
            selq[p * HEAD_PAD + hh, hh * HEAD_PAD + HEAD_DIM + p] = 1.0
            selk[p * HEAD_PAD + hh, hh * HEAD_PAD + HEAD_DIM + 3 + p] = 1.0
            oneq[0, hh * HEAD_PAD + HEAD_DIM + 3 + p] = 1.0
            onek[0, hh * HEAD_PAD + HEAD_DIM + p] = 1.0
    return (jnp.asarray(selq, BF16), jnp.asarray(selk, BF16), jnp.asarray(oneq), jnp.asarray(onek))


def _pre_call(x, mod3, g1, w1, gqa, gka, gqf, gkf, bf_pad, positions):
    bsz, seq, _ = x.shape
    tm = TM_PRE
    half = HEAD_DIM // 2
    inv_freq = ROPE_THETA ** (-jnp.arange(half, dtype=F32) / half)
    invf = jnp.concatenate([inv_freq, inv_freq, jnp.zeros((HEAD_PAD - HEAD_DIM,), F32)]).reshape(1, HEAD_PAD)
    sign = jnp.concatenate([-jnp.ones((half,), F32), jnp.ones((half,), F32),
                            jnp.zeros((HEAD_PAD - HEAD_DIM,), F32)]).reshape(1, HEAD_PAD)
    tri = jnp.asarray(np.tril(np.ones((tm, tm), np.float32)), BF16)
    selq, selk, oneq, onek = _decay_selectors()

    def full(a):
        return pl.BlockSpec(a.shape, lambda b, s: (0,) * a.ndim)

    def tok(width):
        return pl.BlockSpec((1, tm, width), lambda b, s: (b, s, 0))

    consts = (g1, w1, gqa, gka, gqf, gkf, bf_pad)
    tail = (invf, sign, tri, selq, selk, oneq, onek)
    out_w = (SWA_Q_HEADS * HEAD_PAD, SWA_KV_HEADS * HEAD_PAD, SWA_KV_W,
             FOX_HEADS * HEAD_PAD, FOX_HEADS * HEAD_PAD, FOX_W)
    return pl.pallas_call(
        _pre_kernel,
        out_shape=[jax.ShapeDtypeStruct((bsz, seq, w), BF16) for w in out_w],
        grid=(bsz, seq // tm),
        in_specs=[tok(D_MODEL), pl.BlockSpec((1, N_MOD, D_MODEL), lambda b, s: (b, 0, 0))]
        + [full(a) for a in consts] + [tok(1)] + [full(a) for a in tail],
        out_specs=[tok(w) for w in out_w],
        scratch_shapes=[pltpu.VMEM((1, HEAD_PAD), F32)],
        compiler_params=pltpu.CompilerParams(
            dimension_semantics=("parallel", "arbitrary"), vmem_limit_bytes=VMEM_LIMIT),
        name="pre",
    )(x, mod3, *consts, positions.reshape(bsz, seq, 1), *tail)


def _swa_kernel(sink_ref, q_ref, kc_ref, kp_ref, vc_ref, vp_ref, o_ref):
    t_idx = pl.program_id(1)
    tq = q_ref.shape[1]
    nsub = tq // BLOCK
    k_all = jnp.concatenate([kp_ref[0], kc_ref[0]], axis=0)
    v_all = jnp.concatenate([vp_ref[0], vc_ref[0]], axis=0)
    v_swap = jnp.concatenate([v_all[:, HEAD_DIM:], v_all[:, :HEAD_DIM]], axis=1)

    rows = SWA_GROUP * BLOCK
    r = lax.broadcasted_iota(jnp.int32, (rows, 2 * BLOCK), 0)
    kcol = lax.broadcasted_iota(jnp.int32, (rows, 2 * BLOCK), 1)
    rel = (r % BLOCK) + BLOCK - kcol
    in_window = (rel >= 0) & (rel < SWA_WINDOW)
    rgrp = lax.broadcasted_iota(jnp.int32, (rows, 1), 0) // BLOCK
    lane_lo = lax.broadcasted_iota(jnp.int32, (1, HEAD_PAD), 1) < HEAD_DIM

    for j in range(nsub):
        lo_key = jnp.where(t_idx == 0, BLOCK, 0) if j == 0 else 0
        mask = in_window & (kcol >= lo_key)
        vb = v_all[j * BLOCK:(j + 2) * BLOCK, :]
        vsb = v_swap[j * BLOCK:(j + 2) * BLOCK, :]
        for g in range(SWA_KV_HEADS):
            kb = k_all[j * BLOCK:(j + 2) * BLOCK, g * HEAD_PAD:(g + 1) * HEAD_PAD]
            qg = jnp.concatenate(
                [q_ref[0, j * BLOCK:(j + 1) * BLOCK,
                       (g * SWA_GROUP + u) * HEAD_PAD:(g * SWA_GROUP + u + 1) * HEAD_PAD]
                 for u in range(SWA_GROUP)], axis=0)
            s = jnp.where(mask, _dot_nt(qg, kb), NEG_INF)
            sink = jnp.zeros((rows, 1), F32)
            for u in range(SWA_GROUP):
                sink = jnp.where(rgrp == u, sink_ref[g * SWA_GROUP + u], sink)
            m = jnp.maximum(jnp.max(s, axis=-1, keepdims=True), sink)
            p = jnp.exp(s - m)
            denom = jnp.sum(p, axis=-1, keepdims=True) + jnp.exp(sink - m)
            pb = p.astype(BF16)
            inv = 1.0 / denom
            o_nat = _dot(pb, vb) * inv
            o_swp = _dot(pb, vsb) * inv
            o_lo, o_hi = (o_nat, o_swp) if g == 0 else (o_swp, o_nat)
            for w in range(SWA_GROUP // 2):
                ev = o_lo[(2 * w) * BLOCK:(2 * w + 1) * BLOCK, :]
                od = o_hi[(2 * w + 1) * BLOCK:(2 * w + 2) * BLOCK, :]
                pc = g * (SWA_GROUP // 2) + w
                o_ref[0, j * BLOCK:(j + 1) * BLOCK, pc * HEAD_PAD:(pc + 1) * HEAD_PAD] = (
                    jnp.where(lane_lo, ev, od).astype(BF16))


def _swa_call(qa, ka, va, sinks):
    bsz, seq, _ = qa.shape
    tq = TQ_SWA
    per = tq // BLOCK

    def cur(width):
        return pl.BlockSpec((1, tq, width), lambda b, t: (b, t, 0))

    def prev(width):
        return pl.BlockSpec((1, BLOCK, width), lambda b, t: (b, jnp.maximum(t * per - 1, 0), 0))

    return pl.pallas_call(
        _swa_kernel,
        out_shape=jax.ShapeDtypeStruct((bsz, seq, SWA_Q_W), BF16),
        grid=(bsz, seq // tq),
        in_specs=[pl.BlockSpec(memory_space=pltpu.SMEM),
                  cur(SWA_Q_HEADS * HEAD_PAD),
                  cur(SWA_KV_HEADS * HEAD_PAD), prev(SWA_KV_HEADS * HEAD_PAD),
                  cur(SWA_KV_W), prev(SWA_KV_W)],
        out_specs=cur(SWA_Q_W),
        compiler_params=pltpu.CompilerParams(
            dimension_semantics=("parallel", "parallel"), vmem_limit_bytes=VMEM_LIMIT),
        name="swa",
    )(sinks.astype(F32), qa, ka, ka, va, va)


def _fox_kernel(q_ref, k_ref, v_ref, o_ref):
    qi = pl.program_id(2)
    tq = q_ref.shape[1]
    tk = tq
    lane_lo = lax.broadcasted_iota(jnp.int32, (1, HEAD_PAD), 1) < HEAD_DIM
    row = lax.broadcasted_iota(jnp.int32, (tq, tk), 0)
    col = lax.broadcasted_iota(jnp.int32, (tq, tk), 1)
    causal = col <= row
    outs = []
    for e in range(2):
        q = q_ref[0, :, e * HEAD_PAD:(e + 1) * HEAD_PAD]

        def step(j, carry, masked):
            m, l, acc = carry
            start = pl.multiple_of(j * tk, tk)
            k = k_ref[0, pl.ds(start, tk), e * HEAD_PAD:(e + 1) * HEAD_PAD]
            v = v_ref[0, pl.ds(start, tk), :]
            s = _dot_nt(q, k)
            if masked:
                s = jnp.where(causal, s, NEG_INF)
            m_new = jnp.maximum(m, jnp.max(s, axis=-1, keepdims=True))
            alpha = jnp.exp(m - m_new)
            p = jnp.exp(s - m_new)
            l = alpha * l + jnp.sum(p, axis=-1, keepdims=True)
            acc = alpha * acc + _dot(p.astype(BF16), v)
            return m_new, l, acc

        init = (jnp.full((tq, 1), NEG_INF, F32), jnp.zeros((tq, 1), F32),
                jnp.zeros((tq, HEAD_PAD), F32))
        carry = lax.fori_loop(0, qi, functools.partial(step, masked=False), init)
        m, l, acc = step(qi, carry, True)
        outs.append(acc * (1.0 / l))
    o_ref[0] = jnp.where(lane_lo, outs[0], outs[1]).astype(BF16)


def _fox_call(qf, kf, vf):
    bsz, seq, _ = qf.shape
    tq = TQ_FOX
    pairs = FOX_HEADS // 2
    return pl.pallas_call(
        _fox_kernel,
        out_shape=jax.ShapeDtypeStruct((bsz, seq, FOX_W), BF16),
        grid=(bsz, pairs, seq // tq),
        in_specs=[pl.BlockSpec((1, tq, 2 * HEAD_PAD), lambda b, p, t: (b, t, p)),
                  pl.BlockSpec((1, seq, 2 * HEAD_PAD), lambda b, p, t: (b, 0, p)),
                  pl.BlockSpec((1, seq, 2 * HEAD_DIM), lambda b, p, t: (b, 0, p))],
        out_specs=pl.BlockSpec((1, tq, 2 * HEAD_DIM), lambda b, p, t: (b, t, p)),
        compiler_params=pltpu.CompilerParams(
            dimension_semantics=("parallel", "parallel", "arbitrary"),
            vmem_limit_bytes=VMEM_LIMIT),
        name="fox",
    )(qf, kf, vf)


def _post_kernel(x_ref, mod_ref, g1_ref, wg_ref, oa_ref, ob_ref, wa_ref, wb_ref, wo_ref, g2_ref,
                 x1_ref, h2_ref):
    x = x_ref[0]
    mod = mod_ref[0]
    h = _rms_modulate(x, g1_ref[...], mod[1:2, :], mod[0:1, :]).astype(BF16)
    gate_a = jax.nn.sigmoid(_dot(h, wg_ref[:, :D_MODEL]))
    gate_b = jax.nn.sigmoid(_dot(h, wg_ref[:, D_MODEL:]))
    merged = gate_a * _dot(oa_ref[0], wa_ref[...]) + gate_b * _dot(ob_ref[0], wb_ref[...])
    x1 = x + mod[2:3, :] * _dot(merged.astype(BF16), wo_ref[...])
    x1_ref[0] = x1
    h2_ref[0] = _rms_modulate(x1, g2_ref[...], mod[4:5, :], mod[3:4, :]).astype(BF16)


def _post_call(x, mod3, g1, wg, out_a, out_b, wa, wb, wo, g2):
    bsz, seq, _ = x.shape
    tm = TM_PRE

    def full(a):
        return pl.BlockSpec(a.shape, lambda b, s: (0,) * a.ndim)

    def tok(width):
        return pl.BlockSpec((1, tm, width), lambda b, s: (b, s, 0))

    return pl.pallas_call(
        _post_kernel,
        out_shape=[jax.ShapeDtypeStruct((bsz, seq, D_MODEL), F32),
                   jax.ShapeDtypeStruct((bsz, seq, D_MODEL), BF16)],
        grid=(bsz, seq // tm),
        in_specs=[tok(D_MODEL), pl.BlockSpec((1, N_MOD, D_MODEL), lambda b, s: (b, 0, 0)),
                  full(g1), full(wg), tok(SWA_Q_W), tok(FOX_W), full(wa), full(wb), full(wo),
                  full(g2)],
        out_specs=[tok(D_MODEL), tok(D_MODEL)],
        compiler_params=pltpu.CompilerParams(
            dimension_semantics=("parallel", "parallel"), vmem_limit_bytes=VMEM_LIMIT),
        name="post",
    )(x, mod3, g1, wg, out_a, out_b, wa, wb, wo, g2)


def _top16(s):
    cur = s
    rank = jnp.full(s.shape, float(PEER_TOPK), F32)
    vals = []
    for r in range(PEER_TOPK):
        mx = jnp.max(cur, axis=0, keepdims=True)
        vals.append(mx)
        hit = cur == mx
        rank = jnp.where(hit, float(r), rank)
        cur = jnp.where(hit, -3.0e38, cur)
    return vals, rank


def _route_kernel(h_ref, wq_ref, keys_ref, n_ref, c0_ref, r1_ref, e1_ref, qt_ref):
    qt_ref[...] = _dot_nt(wq_ref[...], h_ref[...]).astype(BF16)

    def head(hh, _):
        base = pl.multiple_of(hh * PEER_QDIM, PEER_QDIM)
        q0 = qt_ref[pl.ds(base, PEER_HALF), :]
        q1 = qt_ref[pl.ds(base + PEER_HALF, PEER_HALF), :]
        s0 = _dot(keys_ref[2 * hh], q0)
        s1 = _dot(keys_ref[2 * hh + 1], q1)
        v0, rank0 = _top16(s0)
        v1, rank1 = _top16(s1)
        v1_all = jnp.concatenate(v1, axis=0)
        cand = jnp.concatenate([v0[a] + v1_all for a in range(PEER_TOPK)], axis=0)
        cur = cand
        tau = None
        for r in range(PEER_TOPK):
            tau = jnp.max(cur, axis=0, keepdims=True)
            cur = jnp.where(cur == tau, -3.0e38, cur)
        top = v0[0] + v1[0]
        sel = jnp.where(cand >= tau, 1.0, 0.0)
        z = jnp.sum(sel * jnp.exp(cand - top), axis=0, keepdims=True)
        cnt = jnp.zeros_like(s0)
        for a in range(PEER_TOPK):
            n_a = jnp.sum(sel[a * PEER_TOPK:(a + 1) * PEER_TOPK, :], axis=0, keepdims=True)
            cnt = jnp.where(rank0 == float(a), n_a, cnt)
        rows = pl.ds(pl.multiple_of(hh * PEER_NKEYS, PEER_NKEYS), PEER_NKEYS)
        n_ref[rows, :] = cnt
        c0_ref[rows, :] = jnp.exp(s0 - v0[0]) * (0.5 / z)
        half = pl.ds(pl.multiple_of(hh * (PEER_NKEYS // 2), PEER_NKEYS // 2), PEER_NKEYS // 2)
        r1_ref[half, :] = pltpu.bitcast(rank1.astype(BF16), jnp.uint32)
        e1_ref[half, :] = pltpu.bitcast(jnp.exp(s1 - v1[0]).astype(BF16), jnp.uint32)
        return 0

    lax.fori_loop(0, PEER_HEADS, head, 0)


def _route_call(h2, wq_t, keys):
    t_all = h2.shape[0]
    tt = TT_ROUTE
    rows = PEER_HEADS * PEER_NKEYS
    out = jax.ShapeDtypeStruct((rows, t_all), F32)
    out_b = jax.ShapeDtypeStruct((rows // 2, t_all), jnp.uint32)
    return pl.pallas_call(
        _route_kernel,
        out_shape=[out, out, out_b, out_b],
        grid=(t_all // tt,),
        in_specs=[pl.BlockSpec((tt, D_MODEL), lambda t: (t, 0)),
                  pl.BlockSpec(wq_t.shape, lambda t: (0, 0)),
                  pl.BlockSpec(keys.shape, lambda t: (0, 0, 0))],
        out_specs=[pl.BlockSpec((rows, tt), lambda t: (0, t))] * 2
        + [pl.BlockSpec((rows // 2, tt), lambda t: (0, t))] * 2,
        scratch_shapes=[pltpu.VMEM((PEER_HEADS * PEER_QDIM, tt), BF16)],
        compiler_params=pltpu.CompilerParams(
            dimension_semantics=("parallel",), vmem_limit_bytes=VMEM_LIMIT),
        name="route",
    )(h2, wq_t, keys)


def _peer_kernel(h_ref, u_ref, vt_ref, n_ref, c0_ref, r1_ref, e1_ref, x1_ref, mod_ref,
                 o_ref, acc_ref, nsel_ref, csel_ref, a_ref, w_ref):
    e_idx = pl.program_id(1)
    te, tt = a_ref.shape
    blocks = te // PEER_NKEYS
    assert blocks == 8
    tile3 = (PEER_NKEYS // BF16_ROWS, BF16_ROWS, 128)

    @pl.when(e_idx == 0)
    def _():
        acc_ref[...] = jnp.zeros_like(acc_ref)

    for hh in range(PEER_HEADS):
        grp = pl.ds(pl.multiple_of(hh * PEER_NKEYS + e_idx * blocks, 8), 8)
        nsel_ref[hh * 8:(hh + 1) * 8, :] = n_ref[grp, :]
        csel_ref[hh * 8:(hh + 1) * 8, :] = c0_ref[grp, :]

    a_ref[...] = _dot_nt(u_ref[...], h_ref[...])
    for ii in range(blocks):
        rows = slice(ii * PEER_NKEYS, (ii + 1) * PEER_NKEYS)
        for lc in range(tt // 128):
            lanes = slice(lc * 128, (lc + 1) * 128)
            p = None
            for hh in range(PEER_HEADS):
                n_row = nsel_ref[hh * 8 + ii:hh * 8 + ii + 1, lanes]
                c_row = csel_ref[hh * 8 + ii:hh * 8 + ii + 1, lanes]
                n_b = jnp.broadcast_to(n_row, (BF16_ROWS, 128)).astype(BF16)[None]
                c_b = jnp.broadcast_to(c_row, (BF16_ROWS, 128)).astype(BF16)[None]
                words = slice(hh * (PEER_NKEYS // 2), (hh + 1) * (PEER_NKEYS // 2))
                r1 = pltpu.bitcast(r1_ref[words, lanes], BF16).reshape(tile3)
                e1 = pltpu.bitcast(e1_ref[words, lanes], BF16).reshape(tile3)
                term = jnp.where(r1 < n_b, e1 * c_b, jnp.zeros((), BF16))
                p = term if p is None else p + term
            a = a_ref[rows, lanes]
            gelu2 = a * (1.0 + lax.erf(a * (2.0 ** -0.5)))
            w_ref[rows, lanes] = p.reshape(PEER_NKEYS, 128) * gelu2.astype(BF16)
    acc_ref[...] += _dot(vt_ref[...], w_ref[...])

    @pl.when(e_idx == pl.num_programs(1) - 1)
    def _():
        o_ref[...] = x1_ref[...] + mod_ref[0][5:6, :] * acc_ref[...].T


def _peer_call(h2, u_b, vt_b, n_t, c0_t, r1_t, e1_t, x1, mod3, tiles_per_batch):
    t_all = h2.shape[0]
    tt, te = TT_PEER, TE_PEER
    rows = PEER_HEADS * PEER_NKEYS

    def tab(nrows=rows):
        return pl.BlockSpec((nrows, tt), lambda t, e: (0, t))

    return pl.pallas_call(
        _peer_kernel,
        out_shape=jax.ShapeDtypeStruct((t_all, D_MODEL), F32),
        grid=(t_all // tt, PEER_EXPERTS // te),
        in_specs=[pl.BlockSpec((tt, D_MODEL), lambda t, e: (t, 0)),
                  pl.BlockSpec((te, D_MODEL), lambda t, e: (e, 0)),
                  pl.BlockSpec((D_MODEL, te), lambda t, e: (0, e)),
                  tab(), tab(), tab(rows // 2), tab(rows // 2),
                  pl.BlockSpec((tt, D_MODEL), lambda t, e: (t, 0)),
                  pl.BlockSpec((1, N_MOD, D_MODEL), lambda t, e: (t // tiles_per_batch, 0, 0))],
        out_specs=pl.BlockSpec((tt, D_MODEL), lambda t, e: (t, 0)),
        scratch_shapes=[pltpu.VMEM((D_MODEL, tt), F32),
                        pltpu.VMEM((PEER_HEADS * 8, tt), F32), pltpu.VMEM((PEER_HEADS * 8, tt), F32),
                        pltpu.VMEM((te, tt), F32), pltpu.VMEM((te, tt), BF16)],
        compiler_params=pltpu.CompilerParams(
            dimension_semantics=("parallel", "arbitrary"), vmem_limit_bytes=VMEM_LIMIT),
        name="peer",
    )(h2, u_b, vt_b, n_t, c0_t, r1_t, e1_t, x1, mod3)


def _layer(x, c, positions, w_mod, b_mod, norm1_g, w_in, q_norm_swa, k_norm_swa, sinks,
           q_norm_fox, k_norm_fox, b_forget, w_out_swa, w_out_fox, w_o, norm2_g,
           peer_w_query, peer_sub_keys, peer_u, peer_v):
    bsz, seq, _ = x.shape
    split_at = [int(v) for v in np.cumsum(IN_SPLITS)[:-1]]
    w_qa, w_ka, w_va, w_qf, w_kf, w_vf, w_f, w_ga, w_gb = jnp.split(w_in, split_at, axis=1)
    w1 = jnp.concatenate(
        [_pad_heads_cols(w_qa, SWA_Q_HEADS), _pad_heads_cols(w_ka, SWA_KV_HEADS), w_va,
         _pad_heads_cols(w_qf, FOX_HEADS), _pad_heads_cols(w_kf, FOX_HEADS), w_vf,
         jnp.pad(w_f, ((0, 0), (0, HEAD_PAD - FOX_HEADS)))], axis=1).astype(BF16)
    wg = jnp.concatenate([w_ga, w_gb], axis=1).astype(BF16)
    bf_pad = jnp.pad(b_forget.astype(F32), (0, HEAD_PAD - FOX_HEADS)).reshape(1, HEAD_PAD)
    g1 = norm1_g.astype(F32).reshape(1, D_MODEL)
    g2 = norm2_g.astype(F32).reshape(1, D_MODEL)

    mod3 = _mod_call(c, w_mod, b_mod).reshape(bsz, N_MOD, D_MODEL)

    qa, ka, va, qf, kf, vf = _pre_call(
        x, mod3, g1, w1,
        _pad_head_gain(q_norm_swa, SWA_Q_HEADS), _pad_head_gain(k_norm_swa, SWA_KV_HEADS),
        _pad_head_gain(q_norm_fox, FOX_HEADS), _pad_head_gain(k_norm_fox, FOX_HEADS),
        bf_pad, positions)

    out_a = _swa_call(qa, ka, va, sinks)
    out_b = _fox_call(qf, kf, vf)

    x1, h2 = _post_call(x, mod3, g1, wg, out_a, out_b, w_out_swa.astype(BF16),
                        w_out_fox.astype(BF16), w_o.astype(BF16), g2)

    t_all = bsz * seq
    h2f = h2.reshape(t_all, D_MODEL)
    keys = peer_sub_keys.reshape(PEER_HEADS * 2, PEER_NKEYS, PEER_HALF).astype(BF16)
    n_t, c0_t, r1_t, e1_t = _route_call(h2f, peer_w_query.T.astype(BF16), keys)
    out = _peer_call(h2f, peer_u.astype(BF16), peer_v.T.astype(BF16), n_t, c0_t, r1_t, e1_t,
                     x1.reshape(t_all, D_MODEL), mod3, seq // TT_PEER)
    return out.reshape(bsz, seq, D_MODEL)


def kernel(x, c, positions, w_mod, b_mod, norm1_g, w_in, q_norm_swa, k_norm_swa, sinks, q_norm_fox, k_norm_fox, b_forget, w_out_swa, w_out_fox, w_o, norm2_g, peer_w_query, peer_sub_keys, peer_u, peer_v):
    for l in range(w_mod.shape[0]):
        x = _layer(x, c, positions, w_mod[l], b_mod[l], norm1_g[l], w_in[l], q_norm_swa[l],
                   k_norm_swa[l], sinks[l], q_norm_fox[l], k_norm_fox[l], b_forget[l],
                   w_out_swa[l], w_out_fox[l], w_o[l], norm2_g[l], peer_w_query[l],
                   peer_sub_keys[l], peer_u[l], peer_v[l])
    return x
```

```python
import functools

import jax
import jax.numpy as jnp
import numpy as np
from jax import lax
from jax.experimental import pallas as pl
from jax.experimental.pallas import tpu as pltpu

D_MODEL = 1024
HEAD_DIM = 64
HEAD_PAD = 128
SWA_Q_HEADS = 8
SWA_KV_HEADS = 2
SWA_GROUP = SWA_Q_HEADS // SWA_KV_HEADS
SWA_WINDOW = 128
FOX_HEADS = 8
BLOCK = 128
ROPE_THETA = 10000.0
PEER_HEADS = 8
PEER_NKEYS = 128
PEER_EXPERTS = PEER_NKEYS * PEER_NKEYS
PEER_QDIM = 256
PEER_HALF = PEER_QDIM // 2
PEER_TOPK = 16
N_MOD = 6
EPS = 1e-6
NEG_INF = -1e30

SWA_Q_W = SWA_Q_HEADS * HEAD_DIM
SWA_KV_W = SWA_KV_HEADS * HEAD_DIM
FOX_W = FOX_HEADS * HEAD_DIM
IN_SPLITS = (SWA_Q_W, SWA_KV_W, SWA_KV_W, FOX_W, FOX_W, FOX_W, FOX_HEADS, D_MODEL, D_MODEL)

C_QA = 0
C_KA = C_QA + SWA_Q_HEADS * HEAD_PAD
C_VA = C_KA + SWA_KV_HEADS * HEAD_PAD
C_QF = C_VA + SWA_KV_W
C_KF = C_QF + FOX_HEADS * HEAD_PAD
C_VF = C_KF + FOX_HEADS * HEAD_PAD
C_F = C_VF + FOX_W
C_END = C_F + HEAD_PAD

TM_PRE = 512
TQ_SWA = 512
TQ_FOX = 512
TT_ROUTE = 256
TT_PEER = 512
TE_PEER = 1024
VMEM_LIMIT = 56 * 1024 * 1024

BF16 = jnp.bfloat16
F32 = jnp.float32
BF16_ROWS = 16


def _split3(v):
    hi = v.astype(BF16)
    r1 = v - hi.astype(F32)
    mid = r1.astype(BF16)
    lo = (r1 - mid.astype(F32)).astype(BF16)
    return hi, mid, lo


def _dot_nt(a, b):
    return lax.dot_general(a, b, (((1,), (1,)), ((), ())), preferred_element_type=F32)


def _dot(a, b):
    return jnp.dot(a, b, preferred_element_type=F32)


def _rms_modulate(x, g, scale, shift):
    y = x * lax.rsqrt(jnp.mean(x * x, axis=-1, keepdims=True) + EPS)
    return (y * g) * (1.0 + scale) + shift


def _mod_kernel(c_ref, w_ref, b_ref, o_ref):
    c = c_ref[...]
    a = c * jax.nn.sigmoid(c)
    a_hi, a_mid, a_lo = _split3(a)
    w = w_ref[...]
    w_hi, w_mid, w_lo = _split3(w)
    acc = _dot(a_hi, w_hi) + (_dot(a_hi, w_mid) + _dot(a_mid, w_hi))
    acc = acc + (_dot(a_mid, w_mid) + _dot(a_hi, w_lo) + _dot(a_lo, w_hi))
    o_ref[...] = acc + b_ref[...]


def _mod_call(c, w_mod, b_mod):
    bsz = c.shape[0]
    n = w_mod.shape[1]
    tn = 1024
    return pl.pallas_call(
        _mod_kernel,
        out_shape=jax.ShapeDtypeStruct((bsz, n), F32),
        grid=(n // tn,),
        in_specs=[
            pl.BlockSpec((bsz, D_MODEL), lambda j: (0, 0)),
            pl.BlockSpec((D_MODEL, tn), lambda j: (0, j)),
            pl.BlockSpec((1, tn), lambda j: (0, j)),
        ],
        out_specs=pl.BlockSpec((bsz, tn), lambda j: (0, j)),
        compiler_params=pltpu.CompilerParams(
            dimension_semantics=("parallel",), vmem_limit_bytes=VMEM_LIMIT),
        name="mod",
    )(c, w_mod, b_mod.reshape(1, n))


def _pre_kernel(x_ref, mod_ref, g1_ref, w_ref, gqa_ref, gka_ref, gqf_ref, gkf_ref, bf_ref,
                pos_ref, invf_ref, sign_ref, tri_ref, selq_ref, selk_ref, oneq_ref, onek_ref,
                qa_ref, ka_ref, va_ref, qf_ref, kf_ref, vf_ref, carry_ref):
    s_idx = pl.program_id(1)

    @pl.when(s_idx == 0)
    def _():
        carry_ref[...] = jnp.zeros_like(carry_ref)

    x = x_ref[0]
    mod = mod_ref[0]
    h = _rms_modulate(x, g1_ref[...], mod[1:2, :], mod[0:1, :]).astype(BF16)

    ang = pos_ref[0].astype(F32) * invf_ref[...]
    cos = jnp.cos(ang)
    sin_s = jnp.sin(ang) * sign_ref[...]
    lane = lax.broadcasted_iota(jnp.int32, (1, HEAD_PAD), 1)
    low_half = lane < HEAD_DIM // 2

    def head_norm(c, g):
        return c * lax.rsqrt(jnp.sum(c * c, axis=-1, keepdims=True) * (1.0 / HEAD_DIM) + EPS) * g

    def rope(c):
        partner = jnp.where(low_half, pltpu.roll(c, HEAD_PAD - HEAD_DIM // 2, 1),
                            pltpu.roll(c, HEAD_DIM // 2, 1))
        return c * cos + partner * sin_s

    scale = HEAD_DIM ** -0.5

    for hh in range(SWA_Q_HEADS):
        c = _dot(h, w_ref[:, C_QA + hh * HEAD_PAD:C_QA + (hh + 1) * HEAD_PAD])
        c = rope(head_norm(c, gqa_ref[:, hh * HEAD_PAD:(hh + 1) * HEAD_PAD])) * scale
        qa_ref[0, :, hh * HEAD_PAD:(hh + 1) * HEAD_PAD] = c.astype(BF16)
    for hh in range(SWA_KV_HEADS):
        c = _dot(h, w_ref[:, C_KA + hh * HEAD_PAD:C_KA + (hh + 1) * HEAD_PAD])
        c = rope(head_norm(c, gka_ref[:, hh * HEAD_PAD:(hh + 1) * HEAD_PAD]))
        ka_ref[0, :, hh * HEAD_PAD:(hh + 1) * HEAD_PAD] = c.astype(BF16)
    va_ref[0] = _dot(h, w_ref[:, C_VA:C_VA + SWA_KV_W]).astype(BF16)
    vf_ref[0] = _dot(h, w_ref[:, C_VF:C_VF + FOX_W]).astype(BF16)

    f_logit = _dot(h, w_ref[:, C_F:C_F + HEAD_PAD]) + bf_ref[...]
    log_f = jnp.minimum(f_logit, 0.0) - jnp.log1p(jnp.exp(-jnp.abs(f_logit)))
    l_hi, l_mid, l_lo = _split3(log_f)
    tri = tri_ref[...]
    dcum = (_dot(tri, l_hi) + _dot(tri, l_mid) + _dot(tri, l_lo)) + carry_ref[...]
    tm = dcum.shape[0]
    carry_ref[...] = dcum[tm - 1:tm, :]

    d_hi, d_mid, d_lo = _split3(dcum)
    d3 = jnp.concatenate([d_hi, d_mid, d_lo], axis=1)
    aug_q = _dot(d3, selq_ref[...]) + oneq_ref[...]
    aug_k = onek_ref[...] - _dot(d3, selk_ref[...])

    for hh in range(FOX_HEADS):
        sl = slice(hh * HEAD_PAD, (hh + 1) * HEAD_PAD)
        c = _dot(h, w_ref[:, C_QF + hh * HEAD_PAD:C_QF + (hh + 1) * HEAD_PAD])
        c = head_norm(c, gqf_ref[:, sl]) * scale + aug_q[:, sl]
        qf_ref[0, :, sl] = c.astype(BF16)
        c = _dot(h, w_ref[:, C_KF + hh * HEAD_PAD:C_KF + (hh + 1) * HEAD_PAD])
        c = head_norm(c, gkf_ref[:, sl]) + aug_k[:, sl]
        kf_ref[0, :, sl] = c.astype(BF16)


def _pad_heads_cols(w, nheads):
    k = w.shape[0]
    w = w.reshape(k, nheads, HEAD_DIM)
    w = jnp.pad(w, ((0, 0), (0, 0), (0, HEAD_PAD - HEAD_DIM)))
    return w.reshape(k, nheads * HEAD_PAD)


def _pad_head_gain(g, nheads):
    g = jnp.pad(g.astype(F32), (0, HEAD_PAD - HEAD_DIM))
    return jnp.tile(g, nheads).reshape(1, nheads * HEAD_PAD)


def _decay_selectors():
    selq = np.zeros((3 * HEAD_PAD, FOX_HEADS * HEAD_PAD), np.float32)
    selk = np.zeros((3 * HEAD_PAD, FOX_HEADS * HEAD_PAD), np.float32)
    oneq = np.zeros((1, FOX_HEADS * HEAD_PAD), np.float32)
    onek = np.zeros((1, FOX_HEADS * HEAD_PAD), np.float32)
    for hh in range(FOX_HEADS):
        for p in range(3):
            selq[p * HEAD_PAD + hh, hh * HEAD_PAD + HEAD_DIM + p] = 1.0
            selk[p * HEAD_PAD + hh, hh * HEAD_PAD + HEAD_DIM + 3 + p] = 1.0
            oneq[0, hh * HEAD_PAD + HEAD_DIM + 3 + p] = 1.0
            onek[0, hh * HEAD_PAD + HEAD_DIM + p] = 1.0
    return (jnp.asarray(selq, BF16), jnp.asarray(selk, BF16), jnp.asarray(oneq), jnp.asarray(onek))


def _pre_call(x, mod3, g1, w1, gqa, gka, gqf, gkf, bf_pad, positions):
    bsz, seq, _ = x.shape
    tm = TM_PRE
    half = HEAD_DIM // 2
    inv_freq = ROPE_THETA ** (-jnp.arange(half, dtype=F32) / half)
    invf = jnp.concatenate([inv_freq, inv_freq, jnp.zeros((HEAD_PAD - HEAD_DIM,), F32)]).reshape(1, HEAD_PAD)
    sign = jnp.concatenate([-jnp.ones((half,), F32), jnp.ones((half,), F32),
                            jnp.zeros((HEAD_PAD - HEAD_DIM,), F32)]).reshape(1, HEAD_PAD)
    tri = jnp.asarray(np.tril(np.ones((tm, tm), np.float32)), BF16)
    selq, selk, oneq, onek = _decay_selectors()

    def full(a):
        return pl.BlockSpec(a.shape, lambda b, s: (0,) * a.ndim)

    def tok(width):
        return pl.BlockSpec((1, tm, width), lambda b, s: (b, s, 0))

    consts = (g1, w1, gqa, gka, gqf, gkf, bf_pad)
    tail = (invf, sign, tri, selq, selk, oneq, onek)
    out_w = (SWA_Q_HEADS * HEAD_PAD, SWA_KV_HEADS * HEAD_PAD, SWA_KV_W,
             FOX_HEADS * HEAD_PAD, FOX_HEADS * HEAD_PAD, FOX_W)
    return pl.pallas_call(
        _pre_kernel,
        out_shape=[jax.ShapeDtypeStruct((bsz, seq, w), BF16) for w in out_w],
        grid=(bsz, seq // tm),
        in_specs=[tok(D_MODEL), pl.BlockSpec((1, N_MOD, D_MODEL), lambda b, s: (b, 0, 0))]
        + [full(a) for a in consts] + [tok(1)] + [full(a) for a in tail],
        out_specs=[tok(w) for w in out_w],
        scratch_shapes=[pltpu.VMEM((1, HEAD_PAD), F32)],
        compiler_params=pltpu.CompilerParams(
            dimension_semantics=("parallel", "arbitrary"), vmem_limit_bytes=VMEM_LIMIT),
        name="pre",
    )(x, mod3, *consts, positions.reshape(bsz, seq, 1), *tail)


def _swa_kernel(sink_ref, q_ref, kc_ref, kp_ref, vc_ref, vp_ref, o_ref):
    t_idx = pl.program_id(1)
    tq = q_ref.shape[1]
    nsub = tq // BLOCK
    k_all = jnp.concatenate([kp_ref[0], kc_ref[0]], axis=0)
    v_all = jnp.concatenate([vp_ref[0], vc_ref[0]], axis=0)
    v_swap = jnp.concatenate([v_all[:, HEAD_DIM:], v_all[:, :HEAD_DIM]], axis=1)

    rows = SWA_GROUP * BLOCK
    r = lax.broadcasted_iota(jnp.int32, (rows, 2 * BLOCK), 0)
    kcol = lax.broadcasted_iota(jnp.int32, (rows, 2 * BLOCK), 1)
    rel = (r % BLOCK) + BLOCK - kcol
    in_window = (rel >= 0) & (rel < SWA_WINDOW)
    rgrp = lax.broadcasted_iota(jnp.int32, (rows, 1), 0) // BLOCK
    lane_lo = lax.broadcasted_iota(jnp.int32, (1, HEAD_PAD), 1) < HEAD_DIM

    for j in range(nsub):
        lo_key = jnp.where(t_idx == 0, BLOCK, 0) if j == 0 else 0
        mask = in_window & (kcol >= lo_key)
        vb = v_all[j * BLOCK:(j + 2) * BLOCK, :]
        vsb = v_swap[j * BLOCK:(j + 2) * BLOCK, :]
        for g in range(SWA_KV_HEADS):
            kb = k_all[j * BLOCK:(j + 2) * BLOCK, g * HEAD_PAD:(g + 1) * HEAD_PAD]
            qg = jnp.concatenate(
                [q_ref[0, j * BLOCK:(j + 1) * BLOCK,
                       (g * SWA_GROUP + u) * HEAD_PAD:(g * SWA_GROUP + u + 1) * HEAD_PAD]
                 for u in range(SWA_GROUP)], axis=0)
            s = jnp.where(mask, _dot_nt(qg, kb), NEG_INF)
            sink = jnp.zeros((rows, 1), F32)
            for u in range(SWA_GROUP):
                sink = jnp.where(rgrp == u, sink_ref[g * SWA_GROUP + u], sink)
            m = jnp.maximum(jnp.max(s, axis=-1, keepdims=True), sink)
            p = jnp.exp(s - m)
            denom = jnp.sum(p, axis=-1, keepdims=True) + jnp.exp(sink - m)
            pb = p.astype(BF16)
            inv = 1.0 / denom
            o_nat = _dot(pb, vb) * inv
            o_swp = _dot(pb, vsb) * inv
            o_lo, o_hi = (o_nat, o_swp) if g == 0 else (o_swp, o_nat)
            for w in range(SWA_GROUP // 2):
                ev = o_lo[(2 * w) * BLOCK:(2 * w + 1) * BLOCK, :]
                od = o_hi[(2 * w + 1) * BLOCK:(2 * w + 2) * BLOCK, :]
                pc = g * (SWA_GROUP // 2) + w
                o_ref[0, j * BLOCK:(j + 1) * BLOCK, pc * HEAD_PAD:(pc + 1) * HEAD_PAD] = (
                    jnp.where(lane_lo, ev, od).astype(BF16))


def _swa_call(qa, ka, va, sinks):
    bsz, seq, _ = qa.shape
    tq = TQ_SWA
    per = tq // BLOCK

    def cur(width):
        return pl.BlockSpec((1, tq, width), lambda b, t: (b, t, 0))

    def prev(width):
        return pl.BlockSpec((1, BLOCK, width), lambda b, t: (b, jnp.maximum(t * per - 1, 0), 0))

    return pl.pallas_call(
        _swa_kernel,
        out_shape=jax.ShapeDtypeStruct((bsz, seq, SWA_Q_W), BF16),
        grid=(bsz, seq // tq),
        in_specs=[pl.BlockSpec(memory_space=pltpu.SMEM),
                  cur(SWA_Q_HEADS * HEAD_PAD),
                  cur(SWA_KV_HEADS * HEAD_PAD), prev(SWA_KV_HEADS * HEAD_PAD),
                  cur(SWA_KV_W), prev(SWA_KV_W)],
        out_specs=cur(SWA_Q_W),
        compiler_params=pltpu.CompilerParams(
            dimension_semantics=("parallel", "parallel"), vmem_limit_bytes=VMEM_LIMIT),
        name="swa",
    )(sinks.astype(F32), qa, ka, ka, va, va)


def _fox_kernel(q_ref, k_ref, v_ref, o_ref):
    qi = pl.program_id(2)
    tq = q_ref.shape[1]
    tk = tq
    lane_lo = lax.broadcasted_iota(jnp.int32, (1, HEAD_PAD), 1) < HEAD_DIM
    row = lax.broadcasted_iota(jnp.int32, (tq, tk), 0)
    col = lax.broadcasted_iota(jnp.int32, (tq, tk), 1)
    causal = col <= row
    outs = []
    for e in range(2):
        q = q_ref[0, :, e * HEAD_PAD:(e + 1) * HEAD_PAD]

        def step(j, carry, masked):
            m, l, acc = carry
            start = pl.multiple_of(j * tk, tk)
            k = k_ref[0, pl.ds(start, tk), e * HEAD_PAD:(e + 1) * HEAD_PAD]
            v = v_ref[0, pl.ds(start, tk), :]
            s = _dot_nt(q, k)
            if masked:
                s = jnp.where(causal, s, NEG_INF)
            m_new = jnp.maximum(m, jnp.max(s, axis=-1, keepdims=True))
            alpha = jnp.exp(m - m_new)
            p = jnp.exp(s - m_new)
            l = alpha * l + jnp.sum(p, axis=-1, keepdims=True)
            acc = alpha * acc + _dot(p.astype(BF16), v)
            return m_new, l, acc

        init = (jnp.full((tq, 1), NEG_INF, F32), jnp.zeros((tq, 1), F32),
                jnp.zeros((tq, HEAD_PAD), F32))
        carry = lax.fori_loop(0, qi, functools.partial(step, masked=False), init)
        m, l, acc = step(qi, carry, True)
        outs.append(acc * (1.0 / l))
    o_ref[0] = jnp.where(lane_lo, outs[0], outs[1]).astype(BF16)


def _fox_call(qf, kf, vf):
    bsz, seq, _ = qf.shape
    tq = TQ_FOX
    pairs = FOX_HEADS // 2
    return pl.pallas_call(
        _fox_kernel,
        out_shape=jax.ShapeDtypeStruct((bsz, seq, FOX_W), BF16),
        grid=(bsz, pairs, seq // tq),
        in_specs=[pl.BlockSpec((1, tq, 2 * HEAD_PAD), lambda b, p, t: (b, t, p)),
                  pl.BlockSpec((1, seq, 2 * HEAD_PAD), lambda b, p, t: (b, 0, p)),
                  pl.BlockSpec((1, seq, 2 * HEAD_DIM), lambda b, p, t: (b, 0, p))],
        out_specs=pl.BlockSpec((1, tq, 2 * HEAD_DIM), lambda b, p, t: (b, t, p)),
        compiler_params=pltpu.CompilerParams(
            dimension_semantics=("parallel", "parallel", "arbitrary"),
            vmem_limit_bytes=VMEM_LIMIT),
        name="fox",
    )(qf, kf, vf)


def _post_kernel(x_ref, mod_ref, g1_ref, wg_ref, oa_ref, ob_ref, wa_ref, wb_ref, wo_ref, g2_ref,
                 x1_ref, h2_ref):
    x = x_ref[0]
    mod = mod_ref[0]
    h = _rms_modulate(x, g1_ref[...], mod[1:2, :], mod[0:1, :]).astype(BF16)
    gate_a = jax.nn.sigmoid(_dot(h, wg_ref[:, :D_MODEL]))
    gate_b = jax.nn.sigmoid(_dot(h, wg_ref[:, D_MODEL:]))
    merged = gate_a * _dot(oa_ref[0], wa_ref[...]) + gate_b * _dot(ob_ref[0], wb_ref[...])
    x1 = x + mod[2:3, :] * _dot(merged.astype(BF16), wo_ref[...])
    x1_ref[0] = x1
    h2_ref[0] = _rms_modulate(x1, g2_ref[...], mod[4:5, :], mod[3:4, :]).astype(BF16)


def _post_call(x, mod3, g1, wg, out_a, out_b, wa, wb, wo, g2):
    bsz, seq, _ = x.shape
    tm = TM_PRE

    def full(a):
        return pl.BlockSpec(a.shape, lambda b, s: (0,) * a.ndim)

    def tok(width):
        return pl.BlockSpec((1, tm, width), lambda b, s: (b, s, 0))

    return pl.pallas_call(
        _post_kernel,
        out_shape=[jax.ShapeDtypeStruct((bsz, seq, D_MODEL), F32),
                   jax.ShapeDtypeStruct((bsz, seq, D_MODEL), BF16)],
        grid=(bsz, seq // tm),
        in_specs=[tok(D_MODEL), pl.BlockSpec((1, N_MOD, D_MODEL), lambda b, s: (b, 0, 0)),
                  full(g1), full(wg), tok(SWA_Q_W), tok(FOX_W), full(wa), full(wb), full(wo),
                  full(g2)],
        out_specs=[tok(D_MODEL), tok(D_MODEL)],
        compiler_params=pltpu.CompilerParams(
            dimension_semantics=("parallel", "parallel"), vmem_limit_bytes=VMEM_LIMIT),
        name="post",
    )(x, mod3, g1, wg, out_a, out_b, wa, wb, wo, g2)


def _top16(s):
    cur = s
    rank = jnp.full(s.shape, float(PEER_TOPK), F32)
    vals = []
    for r in range(PEER_TOPK):
        mx = jnp.max(cur, axis=0, keepdims=True)
        vals.append(mx)
        hit = cur == mx
        rank = jnp.where(hit, float(r), rank)
        cur = jnp.where(hit, -3.0e38, cur)
    return vals, rank


def _route_kernel(h_ref, wq_ref, keys_ref, n_ref, c0_ref, r1_ref, e1_ref, qt_ref):
    qt_ref[...] = _dot_nt(wq_ref[...], h_ref[...]).astype(BF16)

    def head(hh, _):
        base = pl.multiple_of(hh * PEER_QDIM, PEER_QDIM)
        q0 = qt_ref[pl.ds(base, PEER_HALF), :]
        q1 = qt_ref[pl.ds(base + PEER_HALF, PEER_HALF), :]
        s0 = _dot(keys_ref[2 * hh], q0)
        s1 = _dot(keys_ref[2 * hh + 1], q1)
        v0, rank0 = _top16(s0)
        v1, rank1 = _top16(s1)
        v1_all = jnp.concatenate(v1, axis=0)
        cand = jnp.concatenate([v0[a] + v1_all for a in range(PEER_TOPK)], axis=0)
        cur = cand
        tau = None
        for r in range(PEER_TOPK):
            tau = jnp.max(cur, axis=0, keepdims=True)
            cur = jnp.where(cur == tau, -3.0e38, cur)
        top = v0[0] + v1[0]
        sel = jnp.where(cand >= tau, 1.0, 0.0)
        z = jnp.sum(sel * jnp.exp(cand - top), axis=0, keepdims=True)
        cnt = jnp.zeros_like(s0)
        for a in range(PEER_TOPK):
            n_a = jnp.sum(sel[a * PEER_TOPK:(a + 1) * PEER_TOPK, :], axis=0, keepdims=True)
            cnt = jnp.where(rank0 == float(a), n_a, cnt)
        rows = pl.ds(pl.multiple_of(hh * PEER_NKEYS, PEER_NKEYS), PEER_NKEYS)
        n_ref[rows, :] = cnt
        c0_ref[rows, :] = jnp.exp(s0 - v0[0]) * (0.5 / z)
        half = pl.ds(pl.multiple_of(hh * (PEER_NKEYS // 2), PEER_NKEYS // 2), PEER_NKEYS // 2)
        r1_ref[half, :] = pltpu.bitcast(rank1.astype(BF16), jnp.uint32)
        e1_ref[half, :] = pltpu.bitcast(jnp.exp(s1 - v1[0]).astype(BF16), jnp.uint32)
        return 0

    lax.fori_loop(0, PEER_HEADS, head, 0)


def _route_call(h2, wq_t, keys):
    t_all = h2.shape[0]
    tt = TT_ROUTE
    rows = PEER_HEADS * PEER_NKEYS
    out = jax.ShapeDtypeStruct((rows, t_all), F32)
    out_b = jax.ShapeDtypeStruct((rows // 2, t_all), jnp.uint32)
    return pl.pallas_call(
        _route_kernel,
        out_shape=[out, out, out_b, out_b],
        grid=(t_all // tt,),
        in_specs=[pl.BlockSpec((tt, D_MODEL), lambda t: (t, 0)),
                  pl.BlockSpec(wq_t.shape, lambda t: (0, 0)),
                  pl.BlockSpec(keys.shape, lambda t: (0, 0, 0))],
        out_specs=[pl.BlockSpec((rows, tt), lambda t: (0, t))] * 2
        + [pl.BlockSpec((rows // 2, tt), lambda t: (0, t))] * 2,
        scratch_shapes=[pltpu.VMEM((PEER_HEADS * PEER_QDIM, tt), BF16)],
        compiler_params=pltpu.CompilerParams(
            dimension_semantics=("parallel",), vmem_limit_bytes=VMEM_LIMIT),
        name="route",
    )(h2, wq_t, keys)


def _peer_kernel(h_ref, u_ref, vt_ref, n_ref, c0_ref, r1_ref, e1_ref, x1_ref, mod_ref,
                 o_ref, acc_ref, nsel_ref, csel_ref, a_ref, w_ref):
    e_idx = pl.program_id(1)
    te, tt = a_ref.shape
    blocks = te // PEER_NKEYS
    assert blocks == 8
    tile3 = (PEER_NKEYS // BF16_ROWS, BF16_ROWS, 128)

    @pl.when(e_idx == 0)
    def _():
        acc_ref[...] = jnp.zeros_like(acc_ref)

    for hh in range(PEER_HEADS):
        grp = pl.ds(pl.multiple_of(hh * PEER_NKEYS + e_idx * blocks, 8), 8)
        nsel_ref[hh * 8:(hh + 1) * 8, :] = n_ref[grp, :]
        csel_ref[hh * 8:(hh + 1) * 8, :] = c0_ref[grp, :]

    a_ref[...] = _dot_nt(u_ref[...], h_ref[...])
    for ii in range(blocks):
        rows = slice(ii * PEER_NKEYS, (ii + 1) * PEER_NKEYS)
        for lc in range(tt // 128):
            lanes = slice(lc * 128, (lc + 1) * 128)
            p = None
            for hh in range(PEER_HEADS):
                n_row = nsel_ref[hh * 8 + ii:hh * 8 + ii + 1, lanes]
                c_row = csel_ref[hh * 8 + ii:hh * 8 + ii + 1, lanes]
                n_b = jnp.broadcast_to(n_row, (BF16_ROWS, 128)).astype(BF16)[None]
                c_b = jnp.broadcast_to(c_row, (BF16_ROWS, 128)).astype(BF16)[None]
                words = slice(hh * (PEER_NKEYS // 2), (hh + 1) * (PEER_NKEYS // 2))
                r1 = pltpu.bitcast(r1_ref[words, lanes], BF16).reshape(tile3)
                e1 = pltpu.bitcast(e1_ref[words, lanes], BF16).reshape(tile3)
                term = jnp.where(r1 < n_b, e1 * c_b, jnp.zeros((), BF16))
                p = term if p is None else p + term
            a = a_ref[rows, lanes]
            gelu2 = a * (1.0 + lax.erf(a * (2.0 ** -0.5)))
            w_ref[rows, lanes] = p.reshape(PEER_NKEYS, 128) * gelu2.astype(BF16)
    acc_ref[...] += _dot(vt_ref[0], w_ref[...])

    @pl.when(e_idx == pl.num_programs(1) - 1)
    def _():
        o_ref[...] = x1_ref[...] + mod_ref[0][5:6, :] * acc_ref[...].T


def _peer_call(h2, u_b, vt_b, n_t, c0_t, r1_t, e1_t, x1, mod3, tiles_per_batch):
    t_all = h2.shape[0]
    tt, te = TT_PEER, TE_PEER
    rows = PEER_HEADS * PEER_NKEYS

    def tab(nrows=rows):
        return pl.BlockSpec((nrows, tt), lambda t, e: (0, t))

    return pl.pallas_call(
        _peer_kernel,
        out_shape=jax.ShapeDtypeStruct((t_all, D_MODEL), F32),
        grid=(t_all // tt, PEER_EXPERTS // te),
        in_specs=[pl.BlockSpec((tt, D_MODEL), lambda t, e: (t, 0)),
                  pl.BlockSpec((te, D_MODEL), lambda t, e: (e, 0)),
                  pl.BlockSpec((1, D_MODEL, te), lambda t, e: (e, 0, 0)),
                  tab(), tab(), tab(rows // 2), tab(rows // 2),
                  pl.BlockSpec((tt, D_MODEL), lambda t, e: (t, 0)),
                  pl.BlockSpec((1, N_MOD, D_MODEL), lambda t, e: (t // tiles_per_batch, 0, 0))],
        out_specs=pl.BlockSpec((tt, D_MODEL), lambda t, e: (t, 0)),
        scratch_shapes=[pltpu.VMEM((D_MODEL, tt), F32),
                        pltpu.VMEM((PEER_HEADS * 8, tt), F32), pltpu.VMEM((PEER_HEADS * 8, tt), F32),
                        pltpu.VMEM((te, tt), F32), pltpu.VMEM((te, tt), BF16)],
        compiler_params=pltpu.CompilerParams(
            dimension_semantics=("parallel", "arbitrary"), vmem_limit_bytes=VMEM_LIMIT),
        name="peer",
    )(h2, u_b, vt_b, n_t, c0_t, r1_t, e1_t, x1, mod3)


def _layer(x, c, positions, w_mod, b_mod, norm1_g, w_in, q_norm_swa, k_norm_swa, sinks,
           q_norm_fox, k_norm_fox, b_forget, w_out_swa, w_out_fox, w_o, norm2_g,
           peer_w_query, peer_sub_keys, peer_u, peer_v):
    bsz, seq, _ = x.shape
    split_at = [int(v) for v in np.cumsum(IN_SPLITS)[:-1]]
    w_qa, w_ka, w_va, w_qf, w_kf, w_vf, w_f, w_ga, w_gb = jnp.split(w_in, split_at, axis=1)
    w1 = jnp.concatenate(
        [_pad_heads_cols(w_qa, SWA_Q_HEADS), _pad_heads_cols(w_ka, SWA_KV_HEADS), w_va,
         _pad_heads_cols(w_qf, FOX_HEADS), _pad_heads_cols(w_kf, FOX_HEADS), w_vf,
         jnp.pad(w_f, ((0, 0), (0, HEAD_PAD - FOX_HEADS)))], axis=1).astype(BF16)
    wg = jnp.concatenate([w_ga, w_gb], axis=1).astype(BF16)
    bf_pad = jnp.pad(b_forget.astype(F32), (0, HEAD_PAD - FOX_HEADS)).reshape(1, HEAD_PAD)
    g1 = norm1_g.astype(F32).reshape(1, D_MODEL)
    g2 = norm2_g.astype(F32).reshape(1, D_MODEL)

    mod3 = _mod_call(c, w_mod, b_mod).reshape(bsz, N_MOD, D_MODEL)

    qa, ka, va, qf, kf, vf = _pre_call(
        x, mod3, g1, w1,
        _pad_head_gain(q_norm_swa, SWA_Q_HEADS), _pad_head_gain(k_norm_swa, SWA_KV_HEADS),
        _pad_head_gain(q_norm_fox, FOX_HEADS), _pad_head_gain(k_norm_fox, FOX_HEADS),
        bf_pad, positions)

    out_a = _swa_call(qa, ka, va, sinks)
    out_b = _fox_call(qf, kf, vf)

    x1, h2 = _post_call(x, mod3, g1, wg, out_a, out_b, w_out_swa.astype(BF16),
                        w_out_fox.astype(BF16), w_o.astype(BF16), g2)

    t_all = bsz * seq
    h2f = h2.reshape(t_all, D_MODEL)
    keys = peer_sub_keys.reshape(PEER_HEADS * 2, PEER_NKEYS, PEER_HALF).astype(BF16)
    n_t, c0_t, r1_t, e1_t = _route_call(h2f, peer_w_query.T.astype(BF16), keys)
    vt_tiles = peer_v.astype(BF16).reshape(PEER_EXPERTS // TE_PEER, TE_PEER, D_MODEL)
    vt_tiles = vt_tiles.transpose(0, 2, 1)
    out = _peer_call(h2f, peer_u.astype(BF16), vt_tiles, n_t, c0_t, r1_t, e1_t,
                     x1.reshape(t_all, D_MODEL), mod3, seq // TT_PEER)
    return out.reshape(bsz, seq, D_MODEL)


def kernel(x, c, positions, w_mod, b_mod, norm1_g, w_in, q_norm_swa, k_norm_swa, sinks, q_norm_fox, k_norm_fox, b_forget, w_out_swa, w_out_fox, w_o, norm2_g, peer_w_query, peer_sub_keys, peer_u, peer_v):
    for l in range(w_mod.shape[0]):
        x = _layer(x, c, positions, w_mod[l], b_mod[l], norm1_g[l], w_in[l], q_norm_swa[l],
                   k_norm_swa[l], sinks[l], q_norm_fox[l], k_norm_fox[l], b_forget[l],
                   w_out_swa[l], w_out_fox[l], w_o[l], norm2_g[l], peer_w_query[l],
                   peer_sub_keys[l], peer_u[l], peer_v[l])
    return x
```

```python
import functools

import jax
import jax.numpy as jnp
import numpy as np
from jax import lax
from jax.experimental import pallas as pl
from jax.experimental.pallas import tpu as pltpu

D_MODEL = 1024
HEAD_DIM = 64
HEAD_PAD = 128
SWA_Q_HEADS = 8
SWA_KV_HEADS = 2
SWA_GROUP = SWA_Q_HEADS // SWA_KV_HEADS
SWA_WINDOW = 128
FOX_HEADS = 8
BLOCK = 128
ROPE_THETA = 10000.0
PEER_HEADS = 8
PEER_NKEYS = 128
PEER_EXPERTS = PEER_NKEYS * PEER_NKEYS
PEER_QDIM = 256
PEER_HALF = PEER_QDIM // 2
PEER_TOPK = 16
N_MOD = 6
EPS = 1e-6
NEG_INF = -1e30

SWA_Q_W = SWA_Q_HEADS * HEAD_DIM
SWA_KV_W = SWA_KV_HEADS * HEAD_DIM
FOX_W = FOX_HEADS * HEAD_DIM
IN_SPLITS = (SWA_Q_W, SWA_KV_W, SWA_KV_W, FOX_W, FOX_W, FOX_W, FOX_HEADS, D_MODEL, D_MODEL)

C_QA = 0
C_KA = C_QA + SWA_Q_HEADS * HEAD_PAD
C_VA = C_KA + SWA_KV_HEADS * HEAD_PAD
C_QF = C_VA + SWA_KV_W
C_KF = C_QF + FOX_HEADS * HEAD_PAD
C_VF = C_KF + FOX_HEADS * HEAD_PAD
C_F = C_VF + FOX_W
C_END = C_F + HEAD_PAD

TM_PRE = 512
TQ_SWA = 512
TQ_FOX = 512
TT_ROUTE = 256
TT_PEER = 512
TE_PEER = 1024
VMEM_LIMIT = 56 * 1024 * 1024

BF16 = jnp.bfloat16
F32 = jnp.float32
BF16_ROWS = 16


def _split3(v):
    hi = v.astype(BF16)
    r1 = v - hi.astype(F32)
    mid = r1.astype(BF16)
    lo = (r1 - mid.astype(F32)).astype(BF16)
    return hi, mid, lo


def _dot_nt(a, b):
    return lax.dot_general(a, b, (((1,), (1,)), ((), ())), preferred_element_type=F32)


def _dot(a, b):
    return jnp.dot(a, b, preferred_element_type=F32)


def _rms_modulate(x, g, scale, shift):
    y = x * lax.rsqrt(jnp.mean(x * x, axis=-1, keepdims=True) + EPS)
    return (y * g) * (1.0 + scale) + shift


def _mod_kernel(c_ref, w_ref, b_ref, o_ref):
    c = c_ref[...]
    a = c * jax.nn.sigmoid(c)
    a_hi, a_mid, a_lo = _split3(a)
    w = w_ref[...]
    w_hi, w_mid, w_lo = _split3(w)
    acc = _dot(a_hi, w_hi) + (_dot(a_hi, w_mid) + _dot(a_mid, w_hi))
    acc = acc + (_dot(a_mid, w_mid) + _dot(a_hi, w_lo) + _dot(a_lo, w_hi))
    o_ref[...] = acc + b_ref[...]


def _mod_call(c, w_mod, b_mod):
    bsz = c.shape[0]
    n = w_mod.shape[1]
    tn = 1024
    return pl.pallas_call(
        _mod_kernel,
        out_shape=jax.ShapeDtypeStruct((bsz, n), F32),
        grid=(n // tn,),
        in_specs=[
            pl.BlockSpec((bsz, D_MODEL), lambda j: (0, 0)),
            pl.BlockSpec((D_MODEL, tn), lambda j: (0, j)),
            pl.BlockSpec((1, tn), lambda j: (0, j)),
        ],
        out_specs=pl.BlockSpec((bsz, tn), lambda j: (0, j)),
        compiler_params=pltpu.CompilerParams(
            dimension_semantics=("parallel",), vmem_limit_bytes=VMEM_LIMIT),
        name="mod",
    )(c, w_mod, b_mod.reshape(1, n))


def _pre_kernel(x_ref, mod_ref, g1_ref, w_ref, gqa_ref, gka_ref, gqf_ref, gkf_ref, bf_ref,
                pos_ref, invf_ref, sign_ref, tri_ref, selq_ref, selk_ref, oneq_ref, onek_ref,
                qa_ref, ka_ref, va_ref, qf_ref, kf_ref, vf_ref, carry_ref):
    s_idx = pl.program_id(1)

    @pl.when(s_idx == 0)
    def _():
        carry_ref[...] = jnp.zeros_like(carry_ref)

    x = x_ref[0]
    mod = mod_ref[0]
    h = _rms_modulate(x, g1_ref[...], mod[1:2, :], mod[0:1, :]).astype(BF16)

    ang = pos_ref[0].astype(F32) * invf_ref[...]
    cos = jnp.cos(ang)
    sin_s = jnp.sin(ang) * sign_ref[...]
    lane = lax.broadcasted_iota(jnp.int32, (1, HEAD_PAD), 1)
    low_half = lane < HEAD_DIM // 2

    def head_norm(c, g):
        return c * lax.rsqrt(jnp.sum(c * c, axis=-1, keepdims=True) * (1.0 / HEAD_DIM) + EPS) * g

    def rope(c):
        partner = jnp.where(low_half, pltpu.roll(c, HEAD_PAD - HEAD_DIM // 2, 1),
                            pltpu.roll(c, HEAD_DIM // 2, 1))
        return c * cos + partner * sin_s

    scale = HEAD_DIM ** -0.5

    for hh in range(SWA_Q_HEADS):
        c = _dot(h, w_ref[:, C_QA + hh * HEAD_PAD:C_QA + (hh + 1) * HEAD_PAD])
        c = rope(head_norm(c, gqa_ref[:, hh * HEAD_PAD:(hh + 1) * HEAD_PAD])) * scale
        qa_ref[0, :, hh * HEAD_PAD:(hh + 1) * HEAD_PAD] = c.astype(BF16)
    for hh in range(SWA_KV_HEADS):
        c = _dot(h, w_ref[:, C_KA + hh * HEAD_PAD:C_KA + (hh + 1) * HEAD_PAD])
        c = rope(head_norm(c, gka_ref[:, hh * HEAD_PAD:(hh + 1) * HEAD_PAD]))
        ka_ref[0, :, hh * HEAD_PAD:(hh + 1) * HEAD_PAD] = c.astype(BF16)
    va_ref[0] = _dot(h, w_ref[:, C_VA:C_VA + SWA_KV_W]).astype(BF16)
    vf_ref[0] = _dot(h, w_ref[:, C_VF:C_VF + FOX_W]).astype(BF16)

    f_logit = _dot(h, w_ref[:, C_F:C_F + HEAD_PAD]) + bf_ref[...]
    log_f = jnp.minimum(f_logit, 0.0) - jnp.log1p(jnp.exp(-jnp.abs(f_logit)))
    l_hi, l_mid, l_lo = _split3(log_f)
    tri = tri_ref[...]
    dcum = (_dot(tri, l_hi) + _dot(tri, l_mid) + _dot(tri, l_lo)) + carry_ref[...]
    tm = dcum.shape[0]
    carry_ref[...] = dcum[tm - 1:tm, :]

    d_hi, d_mid, d_lo = _split3(dcum)
    d3 = jnp.concatenate([d_hi, d_mid, d_lo], axis=1)
    aug_q = _dot(d3, selq_ref[...]) + oneq_ref[...]
    aug_k = onek_ref[...] - _dot(d3, selk_ref[...])

    for hh in range(FOX_HEADS):
        sl = slice(hh * HEAD_PAD, (hh + 1) * HEAD_PAD)
        c = _dot(h, w_ref[:, C_QF + hh * HEAD_PAD:C_QF + (hh + 1) * HEAD_PAD])
        c = head_norm(c, gqf_ref[:, sl]) * scale + aug_q[:, sl]
        qf_ref[0, :, sl] = c.astype(BF16)
        c = _dot(h, w_ref[:, C_KF + hh * HEAD_PAD:C_KF + (hh + 1) * HEAD_PAD])
        c = head_norm(c, gkf_ref[:, sl]) + aug_k[:, sl]
        kf_ref[0, :, sl] = c.astype(BF16)


def _pad_heads_cols(w, nheads):
    k = w.shape[0]
    w = w.reshape(k, nheads, HEAD_DIM)
    w = jnp.pad(w, ((0, 0), (0, 0), (0, HEAD_PAD - HEAD_DIM)))
    return w.reshape(k, nheads * HEAD_PAD)


def _pad_head_gain(g, nheads):
    g = jnp.pad(g.astype(F32), (0, HEAD_PAD - HEAD_DIM))
    return jnp.tile(g, nheads).reshape(1, nheads * HEAD_PAD)


def _decay_selectors():
    selq = np.zeros((3 * HEAD_PAD, FOX_HEADS * HEAD_PAD), np.float32)
    selk = np.zeros((3 * HEAD_PAD, FOX_HEADS * HEAD_PAD), np.float32)
    oneq = np.zeros((1, FOX_HEADS * HEAD_PAD), np.float32)
    onek = np.zeros((1, FOX_HEADS * HEAD_PAD), np.float32)
    for hh in range(FOX_HEADS):
        for p in range(3):
            selq[p * HEAD_PAD + hh, hh * HEAD_PAD + HEAD_DIM + p] = 1.0
            selk[p * HEAD_PAD + hh, hh * HEAD_PAD + HEAD_DIM + 3 + p] = 1.0
            oneq[0, hh * HEAD_PAD + HEAD_DIM + 3 + p] = 1.0
            onek[0, hh * HEAD_PAD + HEAD_DIM + p] = 1.0
    return (jnp.asarray(selq, BF16), jnp.asarray(selk, BF16), jnp.asarray(oneq), jnp.asarray(onek))


def _pre_call(x, mod3, g1, w1, gqa, gka, gqf, gkf, bf_pad, positions):
    bsz, seq, _ = x.shape
    tm = TM_PRE
    half = HEAD_DIM // 2
    inv_freq = ROPE_THETA ** (-jnp.arange(half, dtype=F32) / half)
    invf = jnp.concatenate([inv_freq, inv_freq, jnp.zeros((HEAD_PAD - HEAD_DIM,), F32)]).reshape(1, HEAD_PAD)
    sign = jnp.concatenate([-jnp.ones((half,), F32), jnp.ones((half,), F32),
                            jnp.zeros((HEAD_PAD - HEAD_DIM,), F32)]).reshape(1, HEAD_PAD)
    tri = jnp.asarray(np.tril(np.ones((tm, tm), np.float32)), BF16)
    selq, selk, oneq, onek = _decay_selectors()

    def full(a):
        return pl.BlockSpec(a.shape, lambda b, s: (0,) * a.ndim)

    def tok(width):
        return pl.BlockSpec((1, tm, width), lambda b, s: (b, s, 0))

    consts = (g1, w1, gqa, gka, gqf, gkf, bf_pad)
    tail = (invf, sign, tri, selq, selk, oneq, onek)
    out_w = (SWA_Q_HEADS * HEAD_PAD, SWA_KV_HEADS * HEAD_PAD, SWA_KV_W,
             FOX_HEADS * HEAD_PAD, FOX_HEADS * HEAD_PAD, FOX_W)
    return pl.pallas_call(
        _pre_kernel,
        out_shape=[jax.ShapeDtypeStruct((bsz, seq, w), BF16) for w in out_w],
        grid=(bsz, seq // tm),
        in_specs=[tok(D_MODEL), pl.BlockSpec((1, N_MOD, D_MODEL), lambda b, s: (b, 0, 0))]
        + [full(a) for a in consts] + [tok(1)] + [full(a) for a in tail],
        out_specs=[tok(w) for w in out_w],
        scratch_shapes=[pltpu.VMEM((1, HEAD_PAD), F32)],
        compiler_params=pltpu.CompilerParams(
            dimension_semantics=("parallel", "arbitrary"), vmem_limit_bytes=VMEM_LIMIT),
        name="pre",
    )(x, mod3, *consts, positions.reshape(bsz, seq, 1), *tail)


def _swa_kernel(sink_ref, q_ref, kc_ref, kp_ref, vc_ref, vp_ref, o_ref):
    t_idx = pl.program_id(1)
    tq = q_ref.shape[1]
    nsub = tq // BLOCK
    k_all = jnp.concatenate([kp_ref[0], kc_ref[0]], axis=0)
    v_all = jnp.concatenate([vp_ref[0], vc_ref[0]], axis=0)
    v_swap = jnp.concatenate([v_all[:, HEAD_DIM:], v_all[:, :HEAD_DIM]], axis=1)

    rows = SWA_GROUP * BLOCK
    r = lax.broadcasted_iota(jnp.int32, (rows, 2 * BLOCK), 0)
    kcol = lax.broadcasted_iota(jnp.int32, (rows, 2 * BLOCK), 1)
    rel = (r % BLOCK) + BLOCK - kcol
    in_window = (rel >= 0) & (rel < SWA_WINDOW)
    rgrp = lax.broadcasted_iota(jnp.int32, (rows, 1), 0) // BLOCK
    lane_lo = lax.broadcasted_iota(jnp.int32, (1, HEAD_PAD), 1) < HEAD_DIM

    for j in range(nsub):
        lo_key = jnp.where(t_idx == 0, BLOCK, 0) if j == 0 else 0
        mask = in_window & (kcol >= lo_key)
        vb = v_all[j * BLOCK:(j + 2) * BLOCK, :]
        vsb = v_swap[j * BLOCK:(j + 2) * BLOCK, :]
        for g in range(SWA_KV_HEADS):
            kb = k_all[j * BLOCK:(j + 2) * BLOCK, g * HEAD_PAD:(g + 1) * HEAD_PAD]
            qg = jnp.concatenate(
                [q_ref[0, j * BLOCK:(j + 1) * BLOCK,
                       (g * SWA_GROUP + u) * HEAD_PAD:(g * SWA_GROUP + u + 1) * HEAD_PAD]
                 for u in range(SWA_GROUP)], axis=0)
            s = jnp.where(mask, _dot_nt(qg, kb), NEG_INF)
            sink = jnp.zeros((rows, 1), F32)
            for u in range(SWA_GROUP):
                sink = jnp.where(rgrp == u, sink_ref[g * SWA_GROUP + u], sink)
            m = jnp.maximum(jnp.max(s, axis=-1, keepdims=True), sink)
            p = jnp.exp(s - m)
            denom = jnp.sum(p, axis=-1, keepdims=True) + jnp.exp(sink - m)
            pb = p.astype(BF16)
            inv = 1.0 / denom
            o_nat = _dot(pb, vb) * inv
            o_swp = _dot(pb, vsb) * inv
            o_lo, o_hi = (o_nat, o_swp) if g == 0 else (o_swp, o_nat)
            for w in range(SWA_GROUP // 2):
                ev = o_lo[(2 * w) * BLOCK:(2 * w + 1) * BLOCK, :]
                od = o_hi[(2 * w + 1) * BLOCK:(2 * w + 2) * BLOCK, :]
                pc = g * (SWA_GROUP // 2) + w
                o_ref[0, j * BLOCK:(j + 1) * BLOCK, pc * HEAD_PAD:(pc + 1) * HEAD_PAD] = (
                    jnp.where(lane_lo, ev, od).astype(BF16))


def _swa_call(qa, ka, va, sinks):
    bsz, seq, _ = qa.shape
    tq = TQ_SWA
    per = tq // BLOCK

    def cur(width):
        return pl.BlockSpec((1, tq, width), lambda b, t: (b, t, 0))

    def prev(width):
        return pl.BlockSpec((1, BLOCK, width), lambda b, t: (b, jnp.maximum(t * per - 1, 0), 0))

    return pl.pallas_call(
        _swa_kernel,
        out_shape=jax.ShapeDtypeStruct((bsz, seq, SWA_Q_W), BF16),
        grid=(bsz, seq // tq),
        in_specs=[pl.BlockSpec(memory_space=pltpu.SMEM),
                  cur(SWA_Q_HEADS * HEAD_PAD),
                  cur(SWA_KV_HEADS * HEAD_PAD), prev(SWA_KV_HEADS * HEAD_PAD),
                  cur(SWA_KV_W), prev(SWA_KV_W)],
        out_specs=cur(SWA_Q_W),
        compiler_params=pltpu.CompilerParams(
            dimension_semantics=("parallel", "parallel"), vmem_limit_bytes=VMEM_LIMIT),
        name="swa",
    )(sinks.astype(F32), qa, ka, ka, va, va)


def _fox_kernel(q_ref, k_ref, v_ref, o_ref):
    qi = pl.program_id(2)
    tq = q_ref.shape[1]
    tk = tq
    lane_lo = lax.broadcasted_iota(jnp.int32, (1, HEAD_PAD), 1) < HEAD_DIM
    row = lax.broadcasted_iota(jnp.int32, (tq, tk), 0)
    col = lax.broadcasted_iota(jnp.int32, (tq, tk), 1)
    causal = col <= row
    outs = []
    for e in range(2):
        q = q_ref[0, :, e * HEAD_PAD:(e + 1) * HEAD_PAD]

        def step(j, carry, masked):
            m, l, acc = carry
            start = pl.multiple_of(j * tk, tk)
            k = k_ref[0, pl.ds(start, tk), e * HEAD_PAD:(e + 1) * HEAD_PAD]
            v = v_ref[0, pl.ds(start, tk), :]
            s = _dot_nt(q, k)
            if masked:
                s = jnp.where(causal, s, NEG_INF)
            m_new = jnp.maximum(m, jnp.max(s, axis=-1, keepdims=True))
            alpha = jnp.exp(m - m_new)
            p = jnp.exp(s - m_new)
            l = alpha * l + jnp.sum(p, axis=-1, keepdims=True)
            acc = alpha * acc + _dot(p.astype(BF16), v)
            return m_new, l, acc

        init = (jnp.full((tq, 1), NEG_INF, F32), jnp.zeros((tq, 1), F32),
                jnp.zeros((tq, HEAD_PAD), F32))
        carry = lax.fori_loop(0, qi, functools.partial(step, masked=False), init)
        m, l, acc = step(qi, carry, True)
        outs.append(acc * (1.0 / l))
    o_ref[0] = jnp.where(lane_lo, outs[0], outs[1]).astype(BF16)


def _fox_call(qf, kf, vf):
    bsz, seq, _ = qf.shape
    tq = TQ_FOX
    pairs = FOX_HEADS // 2
    return pl.pallas_call(
        _fox_kernel,
        out_shape=jax.ShapeDtypeStruct((bsz, seq, FOX_W), BF16),
        grid=(bsz, pairs, seq // tq),
        in_specs=[pl.BlockSpec((1, tq, 2 * HEAD_PAD), lambda b, p, t: (b, t, p)),
                  pl.BlockSpec((1, seq, 2 * HEAD_PAD), lambda b, p, t: (b, 0, p)),
                  pl.BlockSpec((1, seq, 2 * HEAD_DIM), lambda b, p, t: (b, 0, p))],
        out_specs=pl.BlockSpec((1, tq, 2 * HEAD_DIM), lambda b, p, t: (b, t, p)),
        compiler_params=pltpu.CompilerParams(
            dimension_semantics=("parallel", "parallel", "arbitrary"),
            vmem_limit_bytes=VMEM_LIMIT),
        name="fox",
    )(qf, kf, vf)


def _post_kernel(x_ref, mod_ref, g1_ref, wg_ref, oa_ref, ob_ref, wa_ref, wb_ref, wo_ref, g2_ref,
                 x1_ref, h2_ref):
    x = x_ref[0]
    mod = mod_ref[0]
    h = _rms_modulate(x, g1_ref[...], mod[1:2, :], mod[0:1, :]).astype(BF16)
    gate_a = jax.nn.sigmoid(_dot(h, wg_ref[:, :D_MODEL]))
    gate_b = jax.nn.sigmoid(_dot(h, wg_ref[:, D_MODEL:]))
    merged = gate_a * _dot(oa_ref[0], wa_ref[...]) + gate_b * _dot(ob_ref[0], wb_ref[...])
    x1 = x + mod[2:3, :] * _dot(merged.astype(BF16), wo_ref[...])
    x1_ref[0] = x1
    h2_ref[0] = _rms_modulate(x1, g2_ref[...], mod[4:5, :], mod[3:4, :]).astype(BF16)


def _post_call(x, mod3, g1, wg, out_a, out_b, wa, wb, wo, g2):
    bsz, seq, _ = x.shape
    tm = TM_PRE

    def full(a):
        return pl.BlockSpec(a.shape, lambda b, s: (0,) * a.ndim)

    def tok(width):
        return pl.BlockSpec((1, tm, width), lambda b, s: (b, s, 0))

    return pl.pallas_call(
        _post_kernel,
        out_shape=[jax.ShapeDtypeStruct((bsz, seq, D_MODEL), F32),
                   jax.ShapeDtypeStruct((bsz, seq, D_MODEL), BF16)],
        grid=(bsz, seq // tm),
        in_specs=[tok(D_MODEL), pl.BlockSpec((1, N_MOD, D_MODEL), lambda b, s: (b, 0, 0)),
                  full(g1), full(wg), tok(SWA_Q_W), tok(FOX_W), full(wa), full(wb), full(wo),
                  full(g2)],
        out_specs=[tok(D_MODEL), tok(D_MODEL)],
        compiler_params=pltpu.CompilerParams(
            dimension_semantics=("parallel", "parallel"), vmem_limit_bytes=VMEM_LIMIT),
        name="post",
    )(x, mod3, g1, wg, out_a, out_b, wa, wb, wo, g2)


def _top16(s):
    cur = s
    rank = jnp.full(s.shape, float(PEER_TOPK), F32)
    vals = []
    for r in range(PEER_TOPK):
        mx = jnp.max(cur, axis=0, keepdims=True)
        vals.append(mx)
        hit = cur == mx
        rank = jnp.where(hit, float(r), rank)
        cur = jnp.where(hit, -3.0e38, cur)
    return vals, rank


CAND_LOW_B = (16, 8, 5, 4)
CAND_HIGH_A = (15, 7, 4)


def _route_chunk(s0, s1):
    v0, rank0 = _top16(s0)
    v1, rank1 = _top16(s1)
    v0_all = jnp.concatenate(v0, axis=0)
    v1_all = jnp.concatenate(v1, axis=0)
    idx = lax.broadcasted_iota(jnp.int32, (PEER_TOPK, 1), 0)
    low = [jnp.where(idx < CAND_LOW_B[a], v0[a] + v1_all, -3.0e38) for a in range(len(CAND_LOW_B))]
    high = [jnp.where((idx >= len(CAND_LOW_B)) & (idx <= CAND_HIGH_A[b]), v0_all + v1[b], -3.0e38)
            for b in range(len(CAND_HIGH_A))]
    cand = jnp.concatenate(low + high, axis=0)
    cur = cand
    tau = None
    for r in range(PEER_TOPK):
        tau = jnp.max(cur, axis=0, keepdims=True)
        cur = jnp.where(cur == tau, -3.0e38, cur)
    top = v0[0] + v1[0]
    sel = jnp.where(cand >= tau, 1.0, 0.0)
    z = jnp.sum(sel * jnp.exp(cand - top), axis=0, keepdims=True)
    n_low = len(CAND_LOW_B)
    n_high = sel[n_low * PEER_TOPK:(n_low + 1) * PEER_TOPK, :]
    for b in range(1, len(CAND_HIGH_A)):
        n_high = n_high + sel[(n_low + b) * PEER_TOPK:(n_low + b + 1) * PEER_TOPK, :]
    cnt = jnp.zeros_like(s0)
    for a in range(PEER_TOPK):
        if a < n_low:
            n_a = jnp.sum(sel[a * PEER_TOPK:(a + 1) * PEER_TOPK, :], axis=0, keepdims=True)
        else:
            n_a = n_high[a:a + 1, :]
        cnt = jnp.where(rank0 == float(a), n_a, cnt)
    return cnt, jnp.exp(s0 - v0[0]) * (0.5 / z), rank1, jnp.exp(s1 - v1[0])


def _route_kernel(h_ref, wq_ref, keys_ref, n_ref, c0_ref, r1_ref, e1_ref, qt_ref, s_ref):
    qt_ref[...] = _dot_nt(wq_ref[...], h_ref[...]).astype(BF16)
    tt = h_ref.shape[0]

    def head(hh, _):
        base = pl.multiple_of(hh * PEER_QDIM, PEER_QDIM)
        s_ref[0] = _dot(keys_ref[2 * hh], qt_ref[pl.ds(base, PEER_HALF), :])
        s_ref[1] = _dot(keys_ref[2 * hh + 1], qt_ref[pl.ds(base + PEER_HALF, PEER_HALF), :])
        rows = pl.ds(pl.multiple_of(hh * PEER_NKEYS, PEER_NKEYS), PEER_NKEYS)
        half = pl.ds(pl.multiple_of(hh * (PEER_NKEYS // 2), PEER_NKEYS // 2), PEER_NKEYS // 2)
        for lc in range(tt // 128):
            lanes = slice(lc * 128, (lc + 1) * 128)
            cnt, c0, rank1, e1 = _route_chunk(s_ref[0, :, lanes], s_ref[1, :, lanes])
            n_ref[lc, rows, :] = cnt
            c0_ref[lc, rows, :] = c0
            r1_ref[lc, half, :] = pltpu.bitcast(rank1.astype(BF16), jnp.uint32)
            e1_ref[lc, half, :] = pltpu.bitcast(e1.astype(BF16), jnp.uint32)
        return 0

    lax.fori_loop(0, PEER_HEADS, head, 0)


def _route_call(h2, wq_t, keys):
    t_all = h2.shape[0]
    tt = TT_ROUTE
    rows = PEER_HEADS * PEER_NKEYS
    out = jax.ShapeDtypeStruct((t_all // 128, rows, 128), F32)
    out_b = jax.ShapeDtypeStruct((t_all // 128, rows // 2, 128), jnp.uint32)
    return pl.pallas_call(
        _route_kernel,
        out_shape=[out, out, out_b, out_b],
        grid=(t_all // tt,),
        in_specs=[pl.BlockSpec((tt, D_MODEL), lambda t: (t, 0)),
                  pl.BlockSpec(wq_t.shape, lambda t: (0, 0)),
                  pl.BlockSpec(keys.shape, lambda t: (0, 0, 0))],
        out_specs=[pl.BlockSpec((tt // 128, rows, 128), lambda t: (t, 0, 0))] * 2
        + [pl.BlockSpec((tt // 128, rows // 2, 128), lambda t: (t, 0, 0))] * 2,
        scratch_shapes=[pltpu.VMEM((PEER_HEADS * PEER_QDIM, tt), BF16),
                        pltpu.VMEM((2, PEER_NKEYS, tt), F32)],
        compiler_params=pltpu.CompilerParams(
            dimension_semantics=("parallel",), vmem_limit_bytes=VMEM_LIMIT),
        name="route",
    )(h2, wq_t, keys)


def _peer_kernel(h_ref, u_ref, vt_ref, n_ref, c0_ref, r1_ref, e1_ref, x1_ref, mod_ref,
                 o_ref, acc_ref, nsel_ref, csel_ref, a_ref, w_ref):
    e_idx = pl.program_id(1)
    te, tt = a_ref.shape
    blocks = te // PEER_NKEYS
    assert blocks == 8
    tile3 = (PEER_NKEYS // BF16_ROWS, BF16_ROWS, 128)

    @pl.when(e_idx == 0)
    def _():
        acc_ref[...] = jnp.zeros_like(acc_ref)

    for lc in range(tt // 128):
        for hh in range(PEER_HEADS):
            grp = pl.ds(pl.multiple_of(hh * PEER_NKEYS + e_idx * blocks, 8), 8)
            nsel_ref[lc, hh * 8:(hh + 1) * 8, :] = n_ref[lc, grp, :]
            csel_ref[lc, hh * 8:(hh + 1) * 8, :] = c0_ref[lc, grp, :]

    a_ref[...] = _dot_nt(u_ref[...], h_ref[...])
    for ii in range(blocks):
        rows = slice(ii * PEER_NKEYS, (ii + 1) * PEER_NKEYS)
        for lc in range(tt // 128):
            lanes = slice(lc * 128, (lc + 1) * 128)
            p = None
            for hh in range(PEER_HEADS):
                n_row = nsel_ref[lc, hh * 8 + ii:hh * 8 + ii + 1, :]
                c_row = csel_ref[lc, hh * 8 + ii:hh * 8 + ii + 1, :]
                n_b = jnp.broadcast_to(n_row, (BF16_ROWS, 128)).astype(BF16)[None]
                c_b = jnp.broadcast_to(c_row, (BF16_ROWS, 128)).astype(BF16)[None]
                words = slice(hh * (PEER_NKEYS // 2), (hh + 1) * (PEER_NKEYS // 2))
                r1 = pltpu.bitcast(r1_ref[lc, words, :], BF16).reshape(tile3)
                e1 = pltpu.bitcast(e1_ref[lc, words, :], BF16).reshape(tile3)
                term = jnp.where(r1 < n_b, e1 * c_b, jnp.zeros((), BF16))
                p = term if p is None else p + term
            a = a_ref[rows, lanes]
            gelu2 = a * (1.0 + lax.erf(a * (2.0 ** -0.5)))
            w_ref[rows, lanes] = p.reshape(PEER_NKEYS, 128) * gelu2.astype(BF16)
    acc_ref[...] += _dot(vt_ref[0], w_ref[...])

    @pl.when(e_idx == pl.num_programs(1) - 1)
    def _():
        o_ref[...] = x1_ref[...] + mod_ref[0][5:6, :] * acc_ref[...].T


def _peer_call(h2, u_b, vt_b, n_t, c0_t, r1_t, e1_t, x1, mod3, tiles_per_batch):
    t_all = h2.shape[0]
    tt, te = TT_PEER, TE_PEER
    rows = PEER_HEADS * PEER_NKEYS

    def tab(nrows=rows):
        return pl.BlockSpec((tt // 128, nrows, 128), lambda t, e: (t, 0, 0))

    return pl.pallas_call(
        _peer_kernel,
        out_shape=jax.ShapeDtypeStruct((t_all, D_MODEL), F32),
        grid=(t_all // tt, PEER_EXPERTS // te),
        in_specs=[pl.BlockSpec((tt, D_MODEL), lambda t, e: (t, 0)),
                  pl.BlockSpec((te, D_MODEL), lambda t, e: (e, 0)),
                  pl.BlockSpec((1, D_MODEL, te), lambda t, e: (e, 0, 0)),
                  tab(), tab(), tab(rows // 2), tab(rows // 2),
                  pl.BlockSpec((tt, D_MODEL), lambda t, e: (t, 0)),
                  pl.BlockSpec((1, N_MOD, D_MODEL), lambda t, e: (t // tiles_per_batch, 0, 0))],
        out_specs=pl.BlockSpec((tt, D_MODEL), lambda t, e: (t, 0)),
        scratch_shapes=[pltpu.VMEM((D_MODEL, tt), F32),
                        pltpu.VMEM((tt // 128, PEER_HEADS * 8, 128), F32),
                        pltpu.VMEM((tt // 128, PEER_HEADS * 8, 128), F32),
                        pltpu.VMEM((te, tt), F32), pltpu.VMEM((te, tt), BF16)],
        compiler_params=pltpu.CompilerParams(
            dimension_semantics=("parallel", "arbitrary"), vmem_limit_bytes=VMEM_LIMIT),
        name="peer",
    )(h2, u_b, vt_b, n_t, c0_t, r1_t, e1_t, x1, mod3)


def _layer(x, c, positions, w_mod, b_mod, norm1_g, w_in, q_norm_swa, k_norm_swa, sinks,
           q_norm_fox, k_norm_fox, b_forget, w_out_swa, w_out_fox, w_o, norm2_g,
           peer_w_query, peer_sub_keys, peer_u, peer_v):
    bsz, seq, _ = x.shape
    split_at = [int(v) for v in np.cumsum(IN_SPLITS)[:-1]]
    w_qa, w_ka, w_va, w_qf, w_kf, w_vf, w_f, w_ga, w_gb = jnp.split(w_in, split_at, axis=1)
    w1 = jnp.concatenate(
        [_pad_heads_cols(w_qa, SWA_Q_HEADS), _pad_heads_cols(w_ka, SWA_KV_HEADS), w_va,
         _pad_heads_cols(w_qf, FOX_HEADS), _pad_heads_cols(w_kf, FOX_HEADS), w_vf,
         jnp.pad(w_f, ((0, 0), (0, HEAD_PAD - FOX_HEADS)))], axis=1).astype(BF16)
    wg = jnp.concatenate([w_ga, w_gb], axis=1).astype(BF16)
    bf_pad = jnp.pad(b_forget.astype(F32), (0, HEAD_PAD - FOX_HEADS)).reshape(1, HEAD_PAD)
    g1 = norm1_g.astype(F32).reshape(1, D_MODEL)
    g2 = norm2_g.astype(F32).reshape(1, D_MODEL)

    mod3 = _mod_call(c, w_mod, b_mod).reshape(bsz, N_MOD, D_MODEL)

    qa, ka, va, qf, kf, vf = _pre_call(
        x, mod3, g1, w1,
        _pad_head_gain(q_norm_swa, SWA_Q_HEADS), _pad_head_gain(k_norm_swa, SWA_KV_HEADS),
        _pad_head_gain(q_norm_fox, FOX_HEADS), _pad_head_gain(k_norm_fox, FOX_HEADS),
        bf_pad, positions)

    out_a = _swa_call(qa, ka, va, sinks)
    out_b = _fox_call(qf, kf, vf)

    x1, h2 = _post_call(x, mod3, g1, wg, out_a, out_b, w_out_swa.astype(BF16),
                        w_out_fox.astype(BF16), w_o.astype(BF16), g2)

    t_all = bsz * seq
    h2f = h2.reshape(t_all, D_MODEL)
    keys = peer_sub_keys.reshape(PEER_HEADS * 2, PEER_NKEYS, PEER_HALF).astype(BF16)
    n_t, c0_t, r1_t, e1_t = _route_call(h2f, peer_w_query.T.astype(BF16), keys)
    vt_tiles = peer_v.astype(BF16).reshape(PEER_EXPERTS // TE_PEER, TE_PEER, D_MODEL)
    vt_tiles = vt_tiles.transpose(0, 2, 1)
    out = _peer_call(h2f, peer_u.astype(BF16), vt_tiles, n_t, c0_t, r1_t, e1_t,
                     x1.reshape(t_all, D_MODEL), mod3, seq // TT_PEER)
    return out.reshape(bsz, seq, D_MODEL)


def kernel(x, c, positions, w_mod, b_mod, norm1_g, w_in, q_norm_swa, k_norm_swa, sinks, q_norm_fox, k_norm_fox, b_forget, w_out_swa, w_out_fox, w_o, norm2_g, peer_w_query, peer_sub_keys, peer_u, peer_v):
    for l in range(w_mod.shape[0]):
        x = _layer(x, c, positions, w_mod[l], b_mod[l], norm1_g[l], w_in[l], q_norm_swa[l],
                   k_norm_swa[l], sinks[l], q_norm_fox[l], k_norm_fox[l], b_forget[l],
                   w_out_swa[l], w_out_fox[l], w_o[l], norm2_g[l], peer_w_query[l],
                   peer_sub_keys[l], peer_u[l], peer_v[l])
    return x
```

```python
import functools

import jax
import jax.numpy as jnp
import numpy as np
from jax import lax
from jax.experimental import pallas as pl
from jax.experimental.pallas import tpu as pltpu

D_MODEL = 1024
HEAD_DIM = 64
HEAD_PAD = 128
SWA_Q_HEADS = 8
SWA_KV_HEADS = 2
SWA_GROUP = SWA_Q_HEADS // SWA_KV_HEADS
SWA_WINDOW = 128
FOX_HEADS = 8
BLOCK = 128
ROPE_THETA = 10000.0
PEER_HEADS = 8
PEER_NKEYS = 128
PEER_EXPERTS = PEER_NKEYS * PEER_NKEYS
PEER_QDIM = 256
PEER_HALF = PEER_QDIM // 2
PEER_TOPK = 16
N_MOD = 6
EPS = 1e-6
NEG_INF = -1e30

SWA_Q_W = SWA_Q_HEADS * HEAD_DIM
SWA_KV_W = SWA_KV_HEADS * HEAD_DIM
FOX_W = FOX_HEADS * HEAD_DIM
IN_SPLITS = (SWA_Q_W, SWA_KV_W, SWA_KV_W, FOX_W, FOX_W, FOX_W, FOX_HEADS, D_MODEL, D_MODEL)

C_QA = 0
C_KA = C_QA + SWA_Q_HEADS * HEAD_PAD
C_VA = C_KA + SWA_KV_HEADS * HEAD_PAD
C_QF = C_VA + SWA_KV_W
C_KF = C_QF + FOX_HEADS * HEAD_PAD
C_VF = C_KF + FOX_HEADS * HEAD_PAD
C_F = C_VF + FOX_W
C_END = C_F + HEAD_PAD

TM_PRE = 512
TQ_SWA = 512
TQ_FOX = 512
FOX_CHAIN_Q = 256
TT_ROUTE = 256
TT_PEER = 512
TE_PEER = 1024
VMEM_LIMIT = 56 * 1024 * 1024

BF16 = jnp.bfloat16
F32 = jnp.float32
BF16_ROWS = 16


def _split3(v):
    hi = v.astype(BF16)
    r1 = v - hi.astype(F32)
    mid = r1.astype(BF16)
    lo = (r1 - mid.astype(F32)).astype(BF16)
    return hi, mid, lo


def _dot_nt(a, b):
    return lax.dot_general(a, b, (((1,), (1,)), ((), ())), preferred_element_type=F32)


def _dot(a, b):
    return jnp.dot(a, b, preferred_element_type=F32)


def _rms_modulate(x, g, scale, shift):
    y = x * lax.rsqrt(jnp.mean(x * x, axis=-1, keepdims=True) + EPS)
    return (y * g) * (1.0 + scale) + shift


def _mod_kernel(c_ref, w_ref, b_ref, o_ref):
    c = c_ref[...]
    a = c * jax.nn.sigmoid(c)
    a_hi, a_mid, a_lo = _split3(a)
    w = w_ref[...]
    w_hi, w_mid, w_lo = _split3(w)
    acc = _dot(a_hi, w_hi) + (_dot(a_hi, w_mid) + _dot(a_mid, w_hi))
    acc = acc + (_dot(a_mid, w_mid) + _dot(a_hi, w_lo) + _dot(a_lo, w_hi))
    o_ref[...] = acc + b_ref[...]


def _mod_call(c, w_mod, b_mod):
    bsz = c.shape[0]
    n = w_mod.shape[1]
    tn = 1024
    return pl.pallas_call(
        _mod_kernel,
        out_shape=jax.ShapeDtypeStruct((bsz, n), F32),
        grid=(n // tn,),
        in_specs=[
            pl.BlockSpec((bsz, D_MODEL), lambda j: (0, 0)),
            pl.BlockSpec((D_MODEL, tn), lambda j: (0, j)),
            pl.BlockSpec((1, tn), lambda j: (0, j)),
        ],
        out_specs=pl.BlockSpec((bsz, tn), lambda j: (0, j)),
        compiler_params=pltpu.CompilerParams(
            dimension_semantics=("parallel",), vmem_limit_bytes=VMEM_LIMIT),
        name="mod",
    )(c, w_mod, b_mod.reshape(1, n))


def _pre_kernel(x_ref, mod_ref, g1_ref, w_ref, gqa_ref, gka_ref, gqf_ref, gkf_ref, bf_ref,
                pos_ref, invf_ref, sign_ref, tri_ref, selq_ref, selk_ref, oneq_ref, onek_ref,
                qa_ref, ka_ref, va_ref, qf_ref, kf_ref, vf_ref, carry_ref):
    s_idx = pl.program_id(1)

    @pl.when(s_idx == 0)
    def _():
        carry_ref[...] = jnp.zeros_like(carry_ref)

    x = x_ref[0]
    mod = mod_ref[0]
    h = _rms_modulate(x, g1_ref[...], mod[1:2, :], mod[0:1, :]).astype(BF16)

    ang = pos_ref[0].astype(F32) * invf_ref[...]
    cos = jnp.cos(ang)
    sin_s = jnp.sin(ang) * sign_ref[...]
    lane = lax.broadcasted_iota(jnp.int32, (1, HEAD_PAD), 1)
    low_half = lane < HEAD_DIM // 2

    def head_norm(c, g):
        return c * lax.rsqrt(jnp.sum(c * c, axis=-1, keepdims=True) * (1.0 / HEAD_DIM) + EPS) * g

    def rope(c):
        partner = jnp.where(low_half, pltpu.roll(c, HEAD_PAD - HEAD_DIM // 2, 1),
                            pltpu.roll(c, HEAD_DIM // 2, 1))
        return c * cos + partner * sin_s

    scale = HEAD_DIM ** -0.5

    for hh in range(SWA_Q_HEADS):
        c = _dot(h, w_ref[:, C_QA + hh * HEAD_PAD:C_QA + (hh + 1) * HEAD_PAD])
        c = rope(head_norm(c, gqa_ref[:, hh * HEAD_PAD:(hh + 1) * HEAD_PAD])) * scale
        qa_ref[0, :, hh * HEAD_PAD:(hh + 1) * HEAD_PAD] = c.astype(BF16)
    for hh in range(SWA_KV_HEADS):
        c = _dot(h, w_ref[:, C_KA + hh * HEAD_PAD:C_KA + (hh + 1) * HEAD_PAD])
        c = rope(head_norm(c, gka_ref[:, hh * HEAD_PAD:(hh + 1) * HEAD_PAD]))
        ka_ref[0, :, hh * HEAD_PAD:(hh + 1) * HEAD_PAD] = c.astype(BF16)
    va_ref[0] = _dot(h, w_ref[:, C_VA:C_VA + SWA_KV_W]).astype(BF16)
    vf_ref[0, 0] = _dot(h, w_ref[:, C_VF:C_VF + FOX_W]).T.astype(BF16)

    f_logit = _dot(h, w_ref[:, C_F:C_F + HEAD_PAD]) + bf_ref[...]
    log_f = jnp.minimum(f_logit, 0.0) - jnp.log1p(jnp.exp(-jnp.abs(f_logit)))
    l_hi, l_mid, l_lo = _split3(log_f)
    tri = tri_ref[...]
    dcum = (_dot(tri, l_hi) + _dot(tri, l_mid) + _dot(tri, l_lo)) + carry_ref[...]
    tm = dcum.shape[0]
    carry_ref[...] = dcum[tm - 1:tm, :]

    d_hi, d_mid, d_lo = _split3(dcum)
    d3 = jnp.concatenate([d_hi, d_mid, d_lo], axis=1)
    aug_q = _dot(d3, selq_ref[...]) + oneq_ref[...]
    aug_k = onek_ref[...] - _dot(d3, selk_ref[...])

    for hh in range(FOX_HEADS):
        sl = slice(hh * HEAD_PAD, (hh + 1) * HEAD_PAD)
        c = _dot(h, w_ref[:, C_QF + hh * HEAD_PAD:C_QF + (hh + 1) * HEAD_PAD])
        c = head_norm(c, gqf_ref[:, sl]) * scale + aug_q[:, sl]
        qf_ref[0, :, sl] = c.astype(BF16)
        c = _dot(h, w_ref[:, C_KF + hh * HEAD_PAD:C_KF + (hh + 1) * HEAD_PAD])
        c = head_norm(c, gkf_ref[:, sl]) + aug_k[:, sl]
        kf_ref[0, :, sl] = c.astype(BF16)


def _pad_heads_cols(w, nheads):
    k = w.shape[0]
    w = w.reshape(k, nheads, HEAD_DIM)
    w = jnp.pad(w, ((0, 0), (0, 0), (0, HEAD_PAD - HEAD_DIM)))
    return w.reshape(k, nheads * HEAD_PAD)


def _pad_head_gain(g, nheads):
    g = jnp.pad(g.astype(F32), (0, HEAD_PAD - HEAD_DIM))
    return jnp.tile(g, nheads).reshape(1, nheads * HEAD_PAD)


def _decay_selectors():
    selq = np.zeros((3 * HEAD_PAD, FOX_HEADS * HEAD_PAD), np.float32)
    selk = np.zeros((3 * HEAD_PAD, FOX_HEADS * HEAD_PAD), np.float32)
    oneq = np.zeros((1, FOX_HEADS * HEAD_PAD), np.float32)
    onek = np.zeros((1, FOX_HEADS * HEAD_PAD), np.float32)
    for hh in range(FOX_HEADS):
        for p in range(3):
            selq[p * HEAD_PAD + hh, hh * HEAD_PAD + HEAD_DIM + p] = 1.0
            selk[p * HEAD_PAD + hh, hh * HEAD_PAD + HEAD_DIM + 3 + p] = 1.0
            oneq[0, hh * HEAD_PAD + HEAD_DIM + 3 + p] = 1.0
            onek[0, hh * HEAD_PAD + HEAD_DIM + p] = 1.0
    return (jnp.asarray(selq, BF16), jnp.asarray(selk, BF16), jnp.asarray(oneq), jnp.asarray(onek))


def _pre_call(x, mod3, g1, w1, gqa, gka, gqf, gkf, bf_pad, positions):
    bsz, seq, _ = x.shape
    tm = TM_PRE
    half = HEAD_DIM // 2
    inv_freq = ROPE_THETA ** (-jnp.arange(half, dtype=F32) / half)
    invf = jnp.concatenate([inv_freq, inv_freq, jnp.zeros((HEAD_PAD - HEAD_DIM,), F32)]).reshape(1, HEAD_PAD)
    sign = jnp.concatenate([-jnp.ones((half,), F32), jnp.ones((half,), F32),
                            jnp.zeros((HEAD_PAD - HEAD_DIM,), F32)]).reshape(1, HEAD_PAD)
    tri = jnp.asarray(np.tril(np.ones((tm, tm), np.float32)), BF16)
    selq, selk, oneq, onek = _decay_selectors()

    def full(a):
        return pl.BlockSpec(a.shape, lambda b, s: (0,) * a.ndim)

    def tok(width):
        return pl.BlockSpec((1, tm, width), lambda b, s: (b, s, 0))

    consts = (g1, w1, gqa, gka, gqf, gkf, bf_pad)
    tail = (invf, sign, tri, selq, selk, oneq, onek)
    out_w = (SWA_Q_HEADS * HEAD_PAD, SWA_KV_HEADS * HEAD_PAD, SWA_KV_W,
             FOX_HEADS * HEAD_PAD, FOX_HEADS * HEAD_PAD)
    assert tm == TQ_FOX
    return pl.pallas_call(
        _pre_kernel,
        out_shape=[jax.ShapeDtypeStruct((bsz, seq, w), BF16) for w in out_w]
        + [jax.ShapeDtypeStruct((bsz, seq // tm, FOX_W, tm), BF16)],
        grid=(bsz, seq // tm),
        in_specs=[tok(D_MODEL), pl.BlockSpec((1, N_MOD, D_MODEL), lambda b, s: (b, 0, 0))]
        + [full(a) for a in consts] + [tok(1)] + [full(a) for a in tail],
        out_specs=[tok(w) for w in out_w]
        + [pl.BlockSpec((1, 1, FOX_W, tm), lambda b, s: (b, s, 0, 0))],
        scratch_shapes=[pltpu.VMEM((1, HEAD_PAD), F32)],
        compiler_params=pltpu.CompilerParams(
            dimension_semantics=("parallel", "arbitrary"), vmem_limit_bytes=VMEM_LIMIT),
        name="pre",
    )(x, mod3, *consts, positions.reshape(bsz, seq, 1), *tail)


def _swa_kernel(sink_ref, q_ref, kc_ref, kp_ref, vc_ref, vp_ref, o_ref):
    t_idx = pl.program_id(1)
    tq = q_ref.shape[1]
    nsub = tq // BLOCK
    k_all = jnp.concatenate([kp_ref[0], kc_ref[0]], axis=0)
    v_all = jnp.concatenate([vp_ref[0], vc_ref[0]], axis=0)
    v_swap = jnp.concatenate([v_all[:, HEAD_DIM:], v_all[:, :HEAD_DIM]], axis=1)

    rows = SWA_GROUP * BLOCK
    r = lax.broadcasted_iota(jnp.int32, (rows, 2 * BLOCK), 0)
    kcol = lax.broadcasted_iota(jnp.int32, (rows, 2 * BLOCK), 1)
    rel = (r % BLOCK) + BLOCK - kcol
    in_window = (rel >= 0) & (rel < SWA_WINDOW)
    rgrp = lax.broadcasted_iota(jnp.int32, (rows, 1), 0) // BLOCK
    lane_lo = lax.broadcasted_iota(jnp.int32, (1, HEAD_PAD), 1) < HEAD_DIM

    for j in range(nsub):
        lo_key = jnp.where(t_idx == 0, BLOCK, 0) if j == 0 else 0
        mask = in_window & (kcol >= lo_key)
        vb = v_all[j * BLOCK:(j + 2) * BLOCK, :]
        vsb = v_swap[j * BLOCK:(j + 2) * BLOCK, :]
        for g in range(SWA_KV_HEADS):
            kb = k_all[j * BLOCK:(j + 2) * BLOCK, g * HEAD_PAD:(g + 1) * HEAD_PAD]
            qg = jnp.concatenate(
                [q_ref[0, j * BLOCK:(j + 1) * BLOCK,
                       (g * SWA_GROUP + u) * HEAD_PAD:(g * SWA_GROUP + u + 1) * HEAD_PAD]
                 for u in range(SWA_GROUP)], axis=0)
            s = jnp.where(mask, _dot_nt(qg, kb), NEG_INF)
            sink = jnp.zeros((rows, 1), F32)
            for u in range(SWA_GROUP):
                sink = jnp.where(rgrp == u, sink_ref[g * SWA_GROUP + u], sink)
            m = jnp.maximum(jnp.max(s, axis=-1, keepdims=True), sink)
            p = jnp.exp(s - m)
            denom = jnp.sum(p, axis=-1, keepdims=True) + jnp.exp(sink - m)
            pb = p.astype(BF16)
            inv = 1.0 / denom
            o_nat = _dot(pb, vb) * inv
            o_swp = _dot(pb, vsb) * inv
            o_lo, o_hi = (o_nat, o_swp) if g == 0 else (o_swp, o_nat)
            for w in range(SWA_GROUP // 2):
                ev = o_lo[(2 * w) * BLOCK:(2 * w + 1) * BLOCK, :]
                od = o_hi[(2 * w + 1) * BLOCK:(2 * w + 2) * BLOCK, :]
                pc = g * (SWA_GROUP // 2) + w
                o_ref[0, j * BLOCK:(j + 1) * BLOCK, pc * HEAD_PAD:(pc + 1) * HEAD_PAD] = (
                    jnp.where(lane_lo, ev, od).astype(BF16))


def _swa_call(qa, ka, va, sinks):
    bsz, seq, _ = qa.shape
    tq = TQ_SWA
    per = tq // BLOCK

    def cur(width):
        return pl.BlockSpec((1, tq, width), lambda b, t: (b, t, 0))

    def prev(width):
        return pl.BlockSpec((1, BLOCK, width), lambda b, t: (b, jnp.maximum(t * per - 1, 0), 0))

    return pl.pallas_call(
        _swa_kernel,
        out_shape=jax.ShapeDtypeStruct((bsz, seq, SWA_Q_W), BF16),
        grid=(bsz, seq // tq),
        in_specs=[pl.BlockSpec(memory_space=pltpu.SMEM),
                  cur(SWA_Q_HEADS * HEAD_PAD),
                  cur(SWA_KV_HEADS * HEAD_PAD), prev(SWA_KV_HEADS * HEAD_PAD),
                  cur(SWA_KV_W), prev(SWA_KV_W)],
        out_specs=cur(SWA_Q_W),
        compiler_params=pltpu.CompilerParams(
            dimension_semantics=("parallel", "parallel"), vmem_limit_bytes=VMEM_LIMIT),
        name="swa",
    )(sinks.astype(F32), qa, ka, ka, va, va)


def _fox_kernel(q_ref, k_ref, vt_ref, o_ref):
    qi = pl.program_id(2)
    tq = q_ref.shape[1]
    tk = tq
    nq = tq // FOX_CHAIN_Q
    key = lax.broadcasted_iota(jnp.int32, (tk, FOX_CHAIN_Q), 0)
    qry = lax.broadcasted_iota(jnp.int32, (tk, FOX_CHAIN_Q), 1)

    def step(j, carry, masked):
        start = pl.multiple_of(j * tk, tk)
        chains = [(e, c) for e in range(2) for c in range(nq)]

        def scores(e, c):
            k = k_ref[0, pl.ds(start, tk), e * HEAD_PAD:(e + 1) * HEAD_PAD]
            q = q_ref[0, c * FOX_CHAIN_Q:(c + 1) * FOX_CHAIN_Q, e * HEAD_PAD:(e + 1) * HEAD_PAD]
            s = _dot_nt(k, q)
            if masked:
                s = jnp.where(key <= qry + c * FOX_CHAIN_Q, s, NEG_INF)
            return s

        out = []
        s_next = scores(*chains[0])
        for idx, (e, c) in enumerate(chains):
            s = s_next
            if idx + 1 < len(chains):
                s_next = scores(*chains[idx + 1])
            m, l, acc = carry[idx]
            m_new = jnp.maximum(m, jnp.max(s, axis=0, keepdims=True))
            alpha = jnp.exp(m - m_new)
            p = jnp.exp(s - m_new)
            l = alpha * l + jnp.sum(p, axis=0, keepdims=True)
            vt = vt_ref[0, j, e * HEAD_DIM:(e + 1) * HEAD_DIM, :]
            acc = alpha * acc + _dot(vt, p.astype(BF16))
            out.append((m_new, l, acc))
        return tuple(out)

    init1 = (jnp.full((1, FOX_CHAIN_Q), NEG_INF, F32), jnp.zeros((1, FOX_CHAIN_Q), F32),
             jnp.zeros((HEAD_DIM, FOX_CHAIN_Q), F32))
    carry = lax.fori_loop(0, qi, functools.partial(step, masked=False), (init1,) * (2 * nq))
    fin = step(qi, carry, True)
    o_t = jnp.concatenate(
        [jnp.concatenate([fin[e * nq + c][2] * (1.0 / fin[e * nq + c][1]) for c in range(nq)], axis=1)
         for e in range(2)], axis=0)
    o_ref[0] = o_t.T.astype(BF16)


def _fox_call(qf, kf, vft):
    bsz, seq, _ = qf.shape
    tq = TQ_FOX
    pairs = FOX_HEADS // 2
    return pl.pallas_call(
        _fox_kernel,
        out_shape=jax.ShapeDtypeStruct((bsz, seq, FOX_W), BF16),
        grid=(bsz, pairs, seq // tq),
        in_specs=[pl.BlockSpec((1, tq, 2 * HEAD_PAD), lambda b, p, t: (b, t, p)),
                  pl.BlockSpec((1, seq, 2 * HEAD_PAD), lambda b, p, t: (b, 0, p)),
                  pl.BlockSpec((1, seq // tq, 2 * HEAD_DIM, tq), lambda b, p, t: (b, 0, p, 0))],
        out_specs=pl.BlockSpec((1, tq, 2 * HEAD_DIM), lambda b, p, t: (b, t, p)),
        compiler_params=pltpu.CompilerParams(
            dimension_semantics=("parallel", "parallel", "arbitrary"),
            vmem_limit_bytes=VMEM_LIMIT),
        name="fox",
    )(qf, kf, vft)


def _post_kernel(x_ref, mod_ref, g1_ref, wg_ref, oa_ref, ob_ref, wa_ref, wb_ref, wo_ref, g2_ref,
                 x1_ref, h2_ref):
    x = x_ref[0]
    mod = mod_ref[0]
    h = _rms_modulate(x, g1_ref[...], mod[1:2, :], mod[0:1, :]).astype(BF16)
    gate_a = jax.nn.sigmoid(_dot(h, wg_ref[:, :D_MODEL]))
    gate_b = jax.nn.sigmoid(_dot(h, wg_ref[:, D_MODEL:]))
    merged = gate_a * _dot(oa_ref[0], wa_ref[...]) + gate_b * _dot(ob_ref[0], wb_ref[...])
    x1 = x + mod[2:3, :] * _dot(merged.astype(BF16), wo_ref[...])
    x1_ref[0] = x1
    h2_ref[0] = _rms_modulate(x1, g2_ref[...], mod[4:5, :], mod[3:4, :]).astype(BF16)


def _post_call(x, mod3, g1, wg, out_a, out_b, wa, wb, wo, g2):
    bsz, seq, _ = x.shape
    tm = TM_PRE

    def full(a):
        return pl.BlockSpec(a.shape, lambda b, s: (0,) * a.ndim)

    def tok(width):
        return pl.BlockSpec((1, tm, width), lambda b, s: (b, s, 0))

    return pl.pallas_call(
        _post_kernel,
        out_shape=[jax.ShapeDtypeStruct((bsz, seq, D_MODEL), F32),
                   jax.ShapeDtypeStruct((bsz, seq, D_MODEL), BF16)],
        grid=(bsz, seq // tm),
        in_specs=[tok(D_MODEL), pl.BlockSpec((1, N_MOD, D_MODEL), lambda b, s: (b, 0, 0)),
                  full(g1), full(wg), tok(SWA_Q_W), tok(FOX_W), full(wa), full(wb), full(wo),
                  full(g2)],
        out_specs=[tok(D_MODEL), tok(D_MODEL)],
        compiler_params=pltpu.CompilerParams(
            dimension_semantics=("parallel", "parallel"), vmem_limit_bytes=VMEM_LIMIT),
        name="post",
    )(x, mod3, g1, wg, out_a, out_b, wa, wb, wo, g2)


def _top16(s):
    cur = s
    rank = jnp.full(s.shape, float(PEER_TOPK), F32)
    vals = []
    for r in range(PEER_TOPK):
        mx = jnp.max(cur, axis=0, keepdims=True)
        vals.append(mx)
        hit = cur == mx
        rank = jnp.where(hit, float(r), rank)
        cur = jnp.where(hit, -3.0e38, cur)
    return vals, rank


CAND_LOW_B = (16, 8, 5, 4)
CAND_HIGH_A = (15, 7, 4)


def _route_chunk(s0, s1):
    v0, rank0 = _top16(s0)
    v1, rank1 = _top16(s1)
    v0_all = jnp.concatenate(v0, axis=0)
    v1_all = jnp.concatenate(v1, axis=0)
    idx = lax.broadcasted_iota(jnp.int32, (PEER_TOPK, 1), 0)
    low = [jnp.where(idx < CAND_LOW_B[a], v0[a] + v1_all, -3.0e38) for a in range(len(CAND_LOW_B))]
    high = [jnp.where((idx >= len(CAND_LOW_B)) & (idx <= CAND_HIGH_A[b]), v0_all + v1[b], -3.0e38)
            for b in range(len(CAND_HIGH_A))]
    cand = jnp.concatenate(low + high, axis=0)
    cur = cand
    tau = None
    for r in range(PEER_TOPK):
        tau = jnp.max(cur, axis=0, keepdims=True)
        cur = jnp.where(cur == tau, -3.0e38, cur)
    top = v0[0] + v1[0]
    sel = jnp.where(cand >= tau, 1.0, 0.0)
    z = jnp.sum(sel * jnp.exp(cand - top), axis=0, keepdims=True)
    n_low = len(CAND_LOW_B)
    n_high = sel[n_low * PEER_TOPK:(n_low + 1) * PEER_TOPK, :]
    for b in range(1, len(CAND_HIGH_A)):
        n_high = n_high + sel[(n_low + b) * PEER_TOPK:(n_low + b + 1) * PEER_TOPK, :]
    cnt = jnp.zeros_like(s0)
    for a in range(PEER_TOPK):
        if a < n_low:
            n_a = jnp.sum(sel[a * PEER_TOPK:(a + 1) * PEER_TOPK, :], axis=0, keepdims=True)
        else:
            n_a = n_high[a:a + 1, :]
        cnt = jnp.where(rank0 == float(a), n_a, cnt)
    return cnt, jnp.exp(s0 - v0[0]) * (0.5 / z), rank1, jnp.exp(s1 - v1[0])


def _route_kernel(h_ref, wq_ref, keys_ref, n_ref, c0_ref, r1_ref, e1_ref, qt_ref, s_ref):
    qt_ref[...] = _dot_nt(wq_ref[...], h_ref[...]).astype(BF16)
    tt = h_ref.shape[0]

    def head(hh, _):
        base = pl.multiple_of(hh * PEER_QDIM, PEER_QDIM)
        s_ref[0] = _dot(keys_ref[2 * hh], qt_ref[pl.ds(base, PEER_HALF), :])
        s_ref[1] = _dot(keys_ref[2 * hh + 1], qt_ref[pl.ds(base + PEER_HALF, PEER_HALF), :])
        rows = pl.ds(pl.multiple_of(hh * PEER_NKEYS, PEER_NKEYS), PEER_NKEYS)
        half = pl.ds(pl.multiple_of(hh * (PEER_NKEYS // 2), PEER_NKEYS // 2), PEER_NKEYS // 2)
        parts = [_route_chunk(s_ref[0, :, lc * 128:(lc + 1) * 128],
                              s_ref[1, :, lc * 128:(lc + 1) * 128]) for lc in range(tt // 128)]
        cnt, c0, rank1, e1 = (jnp.concatenate(t, axis=1) for t in zip(*parts))
        n_ref[rows, :] = cnt
        c0_ref[rows, :] = c0
        r1_ref[half, :] = pltpu.bitcast(rank1.astype(BF16), jnp.uint32)
        e1_ref[half, :] = pltpu.bitcast(e1.astype(BF16), jnp.uint32)
        return 0

    lax.fori_loop(0, PEER_HEADS, head, 0)


def _route_call(h2, wq_t, keys):
    t_all = h2.shape[0]
    tt = TT_ROUTE
    rows = PEER_HEADS * PEER_NKEYS
    out = jax.ShapeDtypeStruct((rows, t_all), F32)
    out_b = jax.ShapeDtypeStruct((rows // 2, t_all), jnp.uint32)
    return pl.pallas_call(
        _route_kernel,
        out_shape=[out, out, out_b, out_b],
        grid=(t_all // tt,),
        in_specs=[pl.BlockSpec((tt, D_MODEL), lambda t: (t, 0)),
                  pl.BlockSpec(wq_t.shape, lambda t: (0, 0)),
                  pl.BlockSpec(keys.shape, lambda t: (0, 0, 0))],
        out_specs=[pl.BlockSpec((rows, tt), lambda t: (0, t))] * 2
        + [pl.BlockSpec((rows // 2, tt), lambda t: (0, t))] * 2,
        scratch_shapes=[pltpu.VMEM((PEER_HEADS * PEER_QDIM, tt), BF16),
                        pltpu.VMEM((2, PEER_NKEYS, tt), F32)],
        compiler_params=pltpu.CompilerParams(
            dimension_semantics=("parallel",), vmem_limit_bytes=VMEM_LIMIT),
        name="route",
    )(h2, wq_t, keys)


def _peer_kernel(h_ref, u_ref, vt_ref, n_ref, c0_ref, r1_ref, e1_ref, x1_ref, mod_ref,
                 o_ref, acc_ref, nsel_ref, csel_ref, a_ref, w_ref):
    e_idx = pl.program_id(1)
    te, tt = a_ref.shape
    blocks = te // PEER_NKEYS
    assert blocks == 8
    tile3 = (PEER_NKEYS // BF16_ROWS, BF16_ROWS, 128)

    @pl.when(e_idx == 0)
    def _():
        acc_ref[...] = jnp.zeros_like(acc_ref)

    for hh in range(PEER_HEADS):
        grp = pl.ds(pl.multiple_of(hh * PEER_NKEYS + e_idx * blocks, 8), 8)
        nsel_ref[hh * 8:(hh + 1) * 8, :] = n_ref[grp, :]
        csel_ref[hh * 8:(hh + 1) * 8, :] = c0_ref[grp, :]

    a_ref[...] = _dot_nt(u_ref[...], h_ref[...])
    for ii in range(blocks):
        rows = slice(ii * PEER_NKEYS, (ii + 1) * PEER_NKEYS)
        for lc in range(tt // 128):
            lanes = slice(lc * 128, (lc + 1) * 128)
            p = None
            for hh in range(PEER_HEADS):
                n_row = nsel_ref[hh * 8 + ii:hh * 8 + ii + 1, lanes]
                c_row = csel_ref[hh * 8 + ii:hh * 8 + ii + 1, lanes]
                n_b = jnp.broadcast_to(n_row, (BF16_ROWS, 128)).astype(BF16)[None]
                c_b = jnp.broadcast_to(c_row, (BF16_ROWS, 128)).astype(BF16)[None]
                words = slice(hh * (PEER_NKEYS // 2), (hh + 1) * (PEER_NKEYS // 2))
                r1 = pltpu.bitcast(r1_ref[words, lanes], BF16).reshape(tile3)
                e1 = pltpu.bitcast(e1_ref[words, lanes], BF16).reshape(tile3)
                term = jnp.where(r1 < n_b, e1 * c_b, jnp.zeros((), BF16))
                p = term if p is None else p + term
            a = a_ref[rows, lanes]
            gelu2 = a * (1.0 + lax.erf(a * (2.0 ** -0.5)))
            w_ref[rows, lanes] = p.reshape(PEER_NKEYS, 128) * gelu2.astype(BF16)
    acc_ref[...] += _dot(vt_ref[0], w_ref[...])

    @pl.when(e_idx == pl.num_programs(1) - 1)
    def _():
        o_ref[...] = x1_ref[...] + mod_ref[0][5:6, :] * acc_ref[...].T


def _peer_call(h2, u_b, vt_b, n_t, c0_t, r1_t, e1_t, x1, mod3, tiles_per_batch):
    t_all = h2.shape[0]
    tt, te = TT_PEER, TE_PEER
    rows = PEER_HEADS * PEER_NKEYS

    def tab(nrows=rows):
        return pl.BlockSpec((nrows, tt), lambda t, e: (0, t))

    return pl.pallas_call(
        _peer_kernel,
        out_shape=jax.ShapeDtypeStruct((t_all, D_MODEL), F32),
        grid=(t_all // tt, PEER_EXPERTS // te),
        in_specs=[pl.BlockSpec((tt, D_MODEL), lambda t, e: (t, 0)),
                  pl.BlockSpec((te, D_MODEL), lambda t, e: (e, 0)),
                  pl.BlockSpec((1, D_MODEL, te), lambda t, e: (e, 0, 0)),
                  tab(), tab(), tab(rows // 2), tab(rows // 2),
                  pl.BlockSpec((tt, D_MODEL), lambda t, e: (t, 0)),
                  pl.BlockSpec((1, N_MOD, D_MODEL), lambda t, e: (t // tiles_per_batch, 0, 0))],
        out_specs=pl.BlockSpec((tt, D_MODEL), lambda t, e: (t, 0)),
        scratch_shapes=[pltpu.VMEM((D_MODEL, tt), F32),
                        pltpu.VMEM((PEER_HEADS * 8, tt), F32), pltpu.VMEM((PEER_HEADS * 8, tt), F32),
                        pltpu.VMEM((te, tt), F32), pltpu.VMEM((te, tt), BF16)],
        compiler_params=pltpu.CompilerParams(
            dimension_semantics=("parallel", "arbitrary"), vmem_limit_bytes=VMEM_LIMIT),
        name="peer",
    )(h2, u_b, vt_b, n_t, c0_t, r1_t, e1_t, x1, mod3)


def _layer(x, c, positions, w_mod, b_mod, norm1_g, w_in, q_norm_swa, k_norm_swa, sinks,
           q_norm_fox, k_norm_fox, b_forget, w_out_swa, w_out_fox, w_o, norm2_g,
           peer_w_query, peer_sub_keys, peer_u, peer_v):
    bsz, seq, _ = x.shape
    split_at = [int(v) for v in np.cumsum(IN_SPLITS)[:-1]]
    w_qa, w_ka, w_va, w_qf, w_kf, w_vf, w_f, w_ga, w_gb = jnp.split(w_in, split_at, axis=1)
    w1 = jnp.concatenate(
        [_pad_heads_cols(w_qa, SWA_Q_HEADS), _pad_heads_cols(w_ka, SWA_KV_HEADS), w_va,
         _pad_heads_cols(w_qf, FOX_HEADS), _pad_heads_cols(w_kf, FOX_HEADS), w_vf,
         jnp.pad(w_f, ((0, 0), (0, HEAD_PAD - FOX_HEADS)))], axis=1).astype(BF16)
    wg = jnp.concatenate([w_ga, w_gb], axis=1).astype(BF16)
    bf_pad = jnp.pad(b_forget.astype(F32), (0, HEAD_PAD - FOX_HEADS)).reshape(1, HEAD_PAD)
    g1 = norm1_g.astype(F32).reshape(1, D_MODEL)
    g2 = norm2_g.astype(F32).reshape(1, D_MODEL)

    mod3 = _mod_call(c, w_mod, b_mod).reshape(bsz, N_MOD, D_MODEL)

    qa, ka, va, qf, kf, vf = _pre_call(
        x, mod3, g1, w1,
        _pad_head_gain(q_norm_swa, SWA_Q_HEADS), _pad_head_gain(k_norm_swa, SWA_KV_HEADS),
        _pad_head_gain(q_norm_fox, FOX_HEADS), _pad_head_gain(k_norm_fox, FOX_HEADS),
        bf_pad, positions)

    out_a = _swa_call(qa, ka, va, sinks)
    out_b = _fox_call(qf, kf, vf)

    x1, h2 = _post_call(x, mod3, g1, wg, out_a, out_b, w_out_swa.astype(BF16),
                        w_out_fox.astype(BF16), w_o.astype(BF16), g2)

    t_all = bsz * seq
    h2f = h2.reshape(t_all, D_MODEL)
    keys = peer_sub_keys.reshape(PEER_HEADS * 2, PEER_NKEYS, PEER_HALF).astype(BF16)
    n_t, c0_t, r1_t, e1_t = _route_call(h2f, peer_w_query.T.astype(BF16), keys)
    vt_tiles = peer_v.astype(BF16).reshape(PEER_EXPERTS // TE_PEER, TE_PEER, D_MODEL)
    vt_tiles = vt_tiles.transpose(0, 2, 1)
    out = _peer_call(h2f, peer_u.astype(BF16), vt_tiles, n_t, c0_t, r1_t, e1_t,
                     x1.reshape(t_all, D_MODEL), mod3, seq // TT_PEER)
    return out.reshape(bsz, seq, D_MODEL)


def kernel(x, c, positions, w_mod, b_mod, norm1_g, w_in, q_norm_swa, k_norm_swa, sinks, q_norm_fox, k_norm_fox, b_forget, w_out_swa, w_out_fox, w_o, norm2_g, peer_w_query, peer_sub_keys, peer_u, peer_v):
    for l in range(w_mod.shape[0]):
        x = _layer(x, c, positions, w_mod[l], b_mod[l], norm1_g[l], w_in[l], q_norm_swa[l],
                   k_norm_swa[l], sinks[l], q_norm_fox[l], k_norm_fox[l], b_forget[l],
                   w_out_swa[l], w_out_fox[l], w_o[l], norm2_g[l], peer_w_query[l],
                   peer_sub_keys[l], peer_u[l], peer_v[l])
    return x
```

```python
import functools

import jax
import jax.numpy as jnp
import numpy as np
from jax import lax
from jax.experimental import pallas as pl
from jax.experimental.pallas import tpu as pltpu

D_MODEL = 1024
HEAD_DIM = 64
HEAD_PAD = 128
SWA_Q_HEADS = 8
SWA_KV_HEADS = 2
SWA_GROUP = SWA_Q_HEADS // SWA_KV_HEADS
SWA_WINDOW = 128
FOX_HEADS = 8
BLOCK = 128
ROPE_THETA = 10000.0
PEER_HEADS = 8
PEER_NKEYS = 128
PEER_EXPERTS = PEER_NKEYS * PEER_NKEYS
PEER_QDIM = 256
PEER_HALF = PEER_QDIM // 2
PEER_TOPK = 16
N_MOD = 6
EPS = 1e-6
NEG_INF = -1e30

SWA_Q_W = SWA_Q_HEADS * HEAD_DIM
SWA_KV_W = SWA_KV_HEADS * HEAD_DIM
FOX_W = FOX_HEADS * HEAD_DIM
IN_SPLITS = (SWA_Q_W, SWA_KV_W, SWA_KV_W, FOX_W, FOX_W, FOX_W, FOX_HEADS, D_MODEL, D_MODEL)

C_QA = 0
C_KA = C_QA + SWA_Q_HEADS * HEAD_PAD
C_VA = C_KA + SWA_KV_HEADS * HEAD_PAD
C_QF = C_VA + SWA_KV_W
C_KF = C_QF + FOX_HEADS * HEAD_PAD
C_VF = C_KF + FOX_HEADS * HEAD_PAD
C_F = C_VF + FOX_W
C_END = C_F + HEAD_PAD

TM_PRE = 512
TQ_SWA = 512
TQ_FOX = 512
FOX_CHAIN_Q = 256
TT_ROUTE = 256
TT_PEER = 512
TE_PEER = 1024
PEER_DMA_SPLIT = 4
VMEM_LIMIT = 56 * 1024 * 1024

BF16 = jnp.bfloat16
F32 = jnp.float32
BF16_ROWS = 16


def _split3(v):
    hi = v.astype(BF16)
    r1 = v - hi.astype(F32)
    mid = r1.astype(BF16)
    lo = (r1 - mid.astype(F32)).astype(BF16)
    return hi, mid, lo


def _dot_nt(a, b):
    return lax.dot_general(a, b, (((1,), (1,)), ((), ())), preferred_element_type=F32)


def _dot(a, b):
    return jnp.dot(a, b, preferred_element_type=F32)


def _rms_modulate(x, g, scale, shift):
    y = x * lax.rsqrt(jnp.mean(x * x, axis=-1, keepdims=True) + EPS)
    return (y * g) * (1.0 + scale) + shift


def _mod_kernel(c_ref, w_ref, b_ref, o_ref):
    c = c_ref[...]
    a = c * jax.nn.sigmoid(c)
    a_hi, a_mid, a_lo = _split3(a)
    w = w_ref[...]
    w_hi, w_mid, w_lo = _split3(w)
    acc = _dot(a_hi, w_hi) + (_dot(a_hi, w_mid) + _dot(a_mid, w_hi))
    acc = acc + (_dot(a_mid, w_mid) + _dot(a_hi, w_lo) + _dot(a_lo, w_hi))
    o_ref[...] = acc + b_ref[...]


def _mod_call(c, w_mod, b_mod):
    bsz = c.shape[0]
    n = w_mod.shape[1]
    tn = 1024
    return pl.pallas_call(
        _mod_kernel,
        out_shape=jax.ShapeDtypeStruct((bsz, n), F32),
        grid=(n // tn,),
        in_specs=[
            pl.BlockSpec((bsz, D_MODEL), lambda j: (0, 0)),
            pl.BlockSpec((D_MODEL, tn), lambda j: (0, j)),
            pl.BlockSpec((1, tn), lambda j: (0, j)),
        ],
        out_specs=pl.BlockSpec((bsz, tn), lambda j: (0, j)),
        compiler_params=pltpu.CompilerParams(
            dimension_semantics=("parallel",), vmem_limit_bytes=VMEM_LIMIT),
        name="mod",
    )(c, w_mod, b_mod.reshape(1, n))


def _pre_kernel(x_ref, mod_ref, g1_ref, w_ref, gqa_ref, gka_ref, gqf_ref, gkf_ref, bf_ref,
                pos_ref, invf_ref, sign_ref, tri_ref, selq_ref, selk_ref, oneq_ref, onek_ref,
                qa_ref, ka_ref, va_ref, qf_ref, kf_ref, vf_ref, carry_ref):
    s_idx = pl.program_id(1)

    @pl.when(s_idx == 0)
    def _():
        carry_ref[...] = jnp.zeros_like(carry_ref)

    x = x_ref[0]
    mod = mod_ref[0]
    h = _rms_modulate(x, g1_ref[...], mod[1:2, :], mod[0:1, :]).astype(BF16)

    ang = pos_ref[0].astype(F32) * invf_ref[...]
    cos = jnp.cos(ang)
    sin_s = jnp.sin(ang) * sign_ref[...]
    lane = lax.broadcasted_iota(jnp.int32, (1, HEAD_PAD), 1)
    low_half = lane < HEAD_DIM // 2

    def head_norm(c, g):
        return c * lax.rsqrt(jnp.sum(c * c, axis=-1, keepdims=True) * (1.0 / HEAD_DIM) + EPS) * g

    def rope(c):
        partner = jnp.where(low_half, pltpu.roll(c, HEAD_PAD - HEAD_DIM // 2, 1),
                            pltpu.roll(c, HEAD_DIM // 2, 1))
        return c * cos + partner * sin_s

    scale = HEAD_DIM ** -0.5

    for hh in range(SWA_Q_HEADS):
        c = _dot(h, w_ref[:, C_QA + hh * HEAD_PAD:C_QA + (hh + 1) * HEAD_PAD])
        c = rope(head_norm(c, gqa_ref[:, hh * HEAD_PAD:(hh + 1) * HEAD_PAD])) * scale
        qa_ref[0, :, hh * HEAD_PAD:(hh + 1) * HEAD_PAD] = c.astype(BF16)
    for hh in range(SWA_KV_HEADS):
        c = _dot(h, w_ref[:, C_KA + hh * HEAD_PAD:C_KA + (hh + 1) * HEAD_PAD])
        c = rope(head_norm(c, gka_ref[:, hh * HEAD_PAD:(hh + 1) * HEAD_PAD]))
        ka_ref[0, :, hh * HEAD_PAD:(hh + 1) * HEAD_PAD] = c.astype(BF16)
    va_ref[0] = _dot(h, w_ref[:, C_VA:C_VA + SWA_KV_W]).astype(BF16)
    vf_ref[0, 0] = _dot(h, w_ref[:, C_VF:C_VF + FOX_W]).T.astype(BF16)

    f_logit = _dot(h, w_ref[:, C_F:C_F + HEAD_PAD]) + bf_ref[...]
    log_f = jnp.minimum(f_logit, 0.0) - jnp.log1p(jnp.exp(-jnp.abs(f_logit)))
    l_hi, l_mid, l_lo = _split3(log_f)
    tri = tri_ref[...]
    dcum = (_dot(tri, l_hi) + _dot(tri, l_mid) + _dot(tri, l_lo)) + carry_ref[...]
    tm = dcum.shape[0]
    carry_ref[...] = dcum[tm - 1:tm, :]

    d_hi, d_mid, d_lo = _split3(dcum)
    d3 = jnp.concatenate([d_hi, d_mid, d_lo], axis=1)
    aug_q = _dot(d3, selq_ref[...]) + oneq_ref[...]
    aug_k = onek_ref[...] - _dot(d3, selk_ref[...])

    for hh in range(FOX_HEADS):
        sl = slice(hh * HEAD_PAD, (hh + 1) * HEAD_PAD)
        c = _dot(h, w_ref[:, C_QF + hh * HEAD_PAD:C_QF + (hh + 1) * HEAD_PAD])
        c = head_norm(c, gqf_ref[:, sl]) * scale + aug_q[:, sl]
        qf_ref[0, :, sl] = c.astype(BF16)
        c = _dot(h, w_ref[:, C_KF + hh * HEAD_PAD:C_KF + (hh + 1) * HEAD_PAD])
        c = head_norm(c, gkf_ref[:, sl]) + aug_k[:, sl]
        kf_ref[0, :, sl] = c.astype(BF16)


def _pad_heads_cols(w, nheads):
    k = w.shape[0]
    w = w.reshape(k, nheads, HEAD_DIM)
    w = jnp.pad(w, ((0, 0), (0, 0), (0, HEAD_PAD - HEAD_DIM)))
    return w.reshape(k, nheads * HEAD_PAD)


def _pad_head_gain(g, nheads):
    g = jnp.pad(g.astype(F32), (0, HEAD_PAD - HEAD_DIM))
    return jnp.tile(g, nheads).reshape(1, nheads * HEAD_PAD)


def _decay_selectors():
    selq = np.zeros((3 * HEAD_PAD, FOX_HEADS * HEAD_PAD), np.float32)
    selk = np.zeros((3 * HEAD_PAD, FOX_HEADS * HEAD_PAD), np.float32)
    oneq = np.zeros((1, FOX_HEADS * HEAD_PAD), np.float32)
    onek = np.zeros((1, FOX_HEADS * HEAD_PAD), np.float32)
    for hh in range(FOX_HEADS):
        for p in range(3):
            selq[p * HEAD_PAD + hh, hh * HEAD_PAD + HEAD_DIM + p] = 1.0
            selk[p * HEAD_PAD + hh, hh * HEAD_PAD + HEAD_DIM + 3 + p] = 1.0
            oneq[0, hh * HEAD_PAD + HEAD_DIM + 3 + p] = 1.0
            onek[0, hh * HEAD_PAD + HEAD_DIM + p] = 1.0
    return (jnp.asarray(selq, BF16), jnp.asarray(selk, BF16), jnp.asarray(oneq), jnp.asarray(onek))


def _pre_call(x, mod3, g1, w1, gqa, gka, gqf, gkf, bf_pad, positions):
    bsz, seq, _ = x.shape
    tm = TM_PRE
    half = HEAD_DIM // 2
    inv_freq = ROPE_THETA ** (-jnp.arange(half, dtype=F32) / half)
    invf = jnp.concatenate([inv_freq, inv_freq, jnp.zeros((HEAD_PAD - HEAD_DIM,), F32)]).reshape(1, HEAD_PAD)
    sign = jnp.concatenate([-jnp.ones((half,), F32), jnp.ones((half,), F32),
                            jnp.zeros((HEAD_PAD - HEAD_DIM,), F32)]).reshape(1, HEAD_PAD)
    tri = jnp.asarray(np.tril(np.ones((tm, tm), np.float32)), BF16)
    selq, selk, oneq, onek = _decay_selectors()

    def full(a):
        return pl.BlockSpec(a.shape, lambda b, s: (0,) * a.ndim)

    def tok(width):
        return pl.BlockSpec((1, tm, width), lambda b, s: (b, s, 0))

    consts = (g1, w1, gqa, gka, gqf, gkf, bf_pad)
    tail = (invf, sign, tri, selq, selk, oneq, onek)
    out_w = (SWA_Q_HEADS * HEAD_PAD, SWA_KV_HEADS * HEAD_PAD, SWA_KV_W,
             FOX_HEADS * HEAD_PAD, FOX_HEADS * HEAD_PAD)
    assert tm == TQ_FOX
    return pl.pallas_call(
        _pre_kernel,
        out_shape=[jax.ShapeDtypeStruct((bsz, seq, w), BF16) for w in out_w]
        + [jax.ShapeDtypeStruct((bsz, seq // tm, FOX_W, tm), BF16)],
        grid=(bsz, seq // tm),
        in_specs=[tok(D_MODEL), pl.BlockSpec((1, N_MOD, D_MODEL), lambda b, s: (b, 0, 0))]
        + [full(a) for a in consts] + [tok(1)] + [full(a) for a in tail],
        out_specs=[tok(w) for w in out_w]
        + [pl.BlockSpec((1, 1, FOX_W, tm), lambda b, s: (b, s, 0, 0))],
        scratch_shapes=[pltpu.VMEM((1, HEAD_PAD), F32)],
        compiler_params=pltpu.CompilerParams(
            dimension_semantics=("parallel", "arbitrary"), vmem_limit_bytes=VMEM_LIMIT),
        name="pre",
    )(x, mod3, *consts, positions.reshape(bsz, seq, 1), *tail)


def _swa_kernel(sink_ref, q_ref, kc_ref, kp_ref, vc_ref, vp_ref, o_ref):
    t_idx = pl.program_id(1)
    tq = q_ref.shape[1]
    nsub = tq // BLOCK
    k_all = jnp.concatenate([kp_ref[0], kc_ref[0]], axis=0)
    v_all = jnp.concatenate([vp_ref[0], vc_ref[0]], axis=0)
    v_swap = jnp.concatenate([v_all[:, HEAD_DIM:], v_all[:, :HEAD_DIM]], axis=1)

    rows = SWA_GROUP * BLOCK
    r = lax.broadcasted_iota(jnp.int32, (rows, 2 * BLOCK), 0)
    kcol = lax.broadcasted_iota(jnp.int32, (rows, 2 * BLOCK), 1)
    rel = (r % BLOCK) + BLOCK - kcol
    in_window = (rel >= 0) & (rel < SWA_WINDOW)
    rgrp = lax.broadcasted_iota(jnp.int32, (rows, 1), 0) // BLOCK
    lane_lo = lax.broadcasted_iota(jnp.int32, (1, HEAD_PAD), 1) < HEAD_DIM

    for j in range(nsub):
        lo_key = jnp.where(t_idx == 0, BLOCK, 0) if j == 0 else 0
        mask = in_window & (kcol >= lo_key)
        vb = v_all[j * BLOCK:(j + 2) * BLOCK, :]
        vsb = v_swap[j * BLOCK:(j + 2) * BLOCK, :]
        for g in range(SWA_KV_HEADS):
            kb = k_all[j * BLOCK:(j + 2) * BLOCK, g * HEAD_PAD:(g + 1) * HEAD_PAD]
            qg = jnp.concatenate(
                [q_ref[0, j * BLOCK:(j + 1) * BLOCK,
                       (g * SWA_GROUP + u) * HEAD_PAD:(g * SWA_GROUP + u + 1) * HEAD_PAD]
                 for u in range(SWA_GROUP)], axis=0)
            s = jnp.where(mask, _dot_nt(qg, kb), NEG_INF)
            sink = jnp.zeros((rows, 1), F32)
            for u in range(SWA_GROUP):
                sink = jnp.where(rgrp == u, sink_ref[g * SWA_GROUP + u], sink)
            m = jnp.maximum(jnp.max(s, axis=-1, keepdims=True), sink)
            p = jnp.exp(s - m)
            denom = jnp.sum(p, axis=-1, keepdims=True) + jnp.exp(sink - m)
            pb = p.astype(BF16)
            inv = 1.0 / denom
            o_nat = _dot(pb, vb) * inv
            o_swp = _dot(pb, vsb) * inv
            o_lo, o_hi = (o_nat, o_swp) if g == 0 else (o_swp, o_nat)
            for w in range(SWA_GROUP // 2):
                ev = o_lo[(2 * w) * BLOCK:(2 * w + 1) * BLOCK, :]
                od = o_hi[(2 * w + 1) * BLOCK:(2 * w + 2) * BLOCK, :]
                pc = g * (SWA_GROUP // 2) + w
                o_ref[0, j * BLOCK:(j + 1) * BLOCK, pc * HEAD_PAD:(pc + 1) * HEAD_PAD] = (
                    jnp.where(lane_lo, ev, od).astype(BF16))


def _swa_call(qa, ka, va, sinks):
    bsz, seq, _ = qa.shape
    tq = TQ_SWA
    per = tq // BLOCK

    def cur(width):
        return pl.BlockSpec((1, tq, width), lambda b, t: (b, t, 0))

    def prev(width):
        return pl.BlockSpec((1, BLOCK, width), lambda b, t: (b, jnp.maximum(t * per - 1, 0), 0))

    return pl.pallas_call(
        _swa_kernel,
        out_shape=jax.ShapeDtypeStruct((bsz, seq, SWA_Q_W), BF16),
        grid=(bsz, seq // tq),
        in_specs=[pl.BlockSpec(memory_space=pltpu.SMEM),
                  cur(SWA_Q_HEADS * HEAD_PAD),
                  cur(SWA_KV_HEADS * HEAD_PAD), prev(SWA_KV_HEADS * HEAD_PAD),
                  cur(SWA_KV_W), prev(SWA_KV_W)],
        out_specs=cur(SWA_Q_W),
        compiler_params=pltpu.CompilerParams(
            dimension_semantics=("parallel", "parallel"), vmem_limit_bytes=VMEM_LIMIT),
        name="swa",
    )(sinks.astype(F32), qa, ka, ka, va, va)


def _fox_kernel(q_ref, k_ref, vt_ref, o_ref):
    qi = pl.program_id(2)
    tq = q_ref.shape[1]
    tk = tq
    nq = tq // FOX_CHAIN_Q
    key = lax.broadcasted_iota(jnp.int32, (tk, FOX_CHAIN_Q), 0)
    qry = lax.broadcasted_iota(jnp.int32, (tk, FOX_CHAIN_Q), 1)

    def step(j, carry, masked):
        start = pl.multiple_of(j * tk, tk)
        chains = [(e, c) for e in range(2) for c in range(nq)]

        def scores(e, c):
            k = k_ref[0, pl.ds(start, tk), e * HEAD_PAD:(e + 1) * HEAD_PAD]
            q = q_ref[0, c * FOX_CHAIN_Q:(c + 1) * FOX_CHAIN_Q, e * HEAD_PAD:(e + 1) * HEAD_PAD]
            s = _dot_nt(k, q)
            if masked:
                s = jnp.where(key <= qry + c * FOX_CHAIN_Q, s, NEG_INF)
            return s

        out = []
        s_next = scores(*chains[0])
        for idx, (e, c) in enumerate(chains):
            s = s_next
            if idx + 1 < len(chains):
                s_next = scores(*chains[idx + 1])
            m, l, acc = carry[idx]
            m_new = jnp.maximum(m, jnp.max(s, axis=0, keepdims=True))
            alpha = jnp.exp(m - m_new)
            p = jnp.exp(s - m_new)
            l = alpha * l + jnp.sum(p, axis=0, keepdims=True)
            vt = vt_ref[0, j, e * HEAD_DIM:(e + 1) * HEAD_DIM, :]
            acc = alpha * acc + _dot(vt, p.astype(BF16))
            out.append((m_new, l, acc))
        return tuple(out)

    init1 = (jnp.full((1, FOX_CHAIN_Q), NEG_INF, F32), jnp.zeros((1, FOX_CHAIN_Q), F32),
             jnp.zeros((HEAD_DIM, FOX_CHAIN_Q), F32))
    carry = lax.fori_loop(0, qi, functools.partial(step, masked=False), (init1,) * (2 * nq))
    fin = step(qi, carry, True)
    o_t = jnp.concatenate(
        [jnp.concatenate([fin[e * nq + c][2] * (1.0 / fin[e * nq + c][1]) for c in range(nq)], axis=1)
         for e in range(2)], axis=0)
    o_ref[0] = o_t.T.astype(BF16)


def _fox_call(qf, kf, vft):
    bsz, seq, _ = qf.shape
    tq = TQ_FOX
    pairs = FOX_HEADS // 2
    return pl.pallas_call(
        _fox_kernel,
        out_shape=jax.ShapeDtypeStruct((bsz, seq, FOX_W), BF16),
        grid=(bsz, pairs, seq // tq),
        in_specs=[pl.BlockSpec((1, tq, 2 * HEAD_PAD), lambda b, p, t: (b, t, p)),
                  pl.BlockSpec((1, seq, 2 * HEAD_PAD), lambda b, p, t: (b, 0, p)),
                  pl.BlockSpec((1, seq // tq, 2 * HEAD_DIM, tq), lambda b, p, t: (b, 0, p, 0))],
        out_specs=pl.BlockSpec((1, tq, 2 * HEAD_DIM), lambda b, p, t: (b, t, p)),
        compiler_params=pltpu.CompilerParams(
            dimension_semantics=("parallel", "parallel", "arbitrary"),
            vmem_limit_bytes=VMEM_LIMIT),
        name="fox",
    )(qf, kf, vft)


def _post_kernel(x_ref, mod_ref, g1_ref, wg_ref, oa_ref, ob_ref, wa_ref, wb_ref, wo_ref, g2_ref,
                 x1_ref, h2_ref):
    x = x_ref[0]
    mod = mod_ref[0]
    h = _rms_modulate(x, g1_ref[...], mod[1:2, :], mod[0:1, :]).astype(BF16)
    gate_a = jax.nn.sigmoid(_dot(h, wg_ref[:, :D_MODEL]))
    gate_b = jax.nn.sigmoid(_dot(h, wg_ref[:, D_MODEL:]))
    merged = gate_a * _dot(oa_ref[0], wa_ref[...]) + gate_b * _dot(ob_ref[0], wb_ref[...])
    x1 = x + mod[2:3, :] * _dot(merged.astype(BF16), wo_ref[...])
    x1_ref[0] = x1
    h2_ref[0] = _rms_modulate(x1, g2_ref[...], mod[4:5, :], mod[3:4, :]).astype(BF16)


def _post_call(x, mod3, g1, wg, out_a, out_b, wa, wb, wo, g2):
    bsz, seq, _ = x.shape
    tm = TM_PRE

    def full(a):
        return pl.BlockSpec(a.shape, lambda b, s: (0,) * a.ndim)

    def tok(width):
        return pl.BlockSpec((1, tm, width), lambda b, s: (b, s, 0))

    return pl.pallas_call(
        _post_kernel,
        out_shape=[jax.ShapeDtypeStruct((bsz, seq, D_MODEL), F32),
                   jax.ShapeDtypeStruct((bsz, seq, D_MODEL), BF16)],
        grid=(bsz, seq // tm),
        in_specs=[tok(D_MODEL), pl.BlockSpec((1, N_MOD, D_MODEL), lambda b, s: (b, 0, 0)),
                  full(g1), full(wg), tok(SWA_Q_W), tok(FOX_W), full(wa), full(wb), full(wo),
                  full(g2)],
        out_specs=[tok(D_MODEL), tok(D_MODEL)],
        compiler_params=pltpu.CompilerParams(
            dimension_semantics=("parallel", "parallel"), vmem_limit_bytes=VMEM_LIMIT),
        name="post",
    )(x, mod3, g1, wg, out_a, out_b, wa, wb, wo, g2)


def _top16(s):
    cur = s
    rank = jnp.full(s.shape, float(PEER_TOPK), F32)
    vals = []
    for r in range(PEER_TOPK):
        mx = jnp.max(cur, axis=0, keepdims=True)
        vals.append(mx)
        hit = cur == mx
        rank = jnp.where(hit, float(r), rank)
        cur = jnp.where(hit, -3.0e38, cur)
    return vals, rank


CAND_LOW_B = (16, 8, 5, 4)
CAND_HIGH_A = (15, 7, 4)


def _route_chunk(s0, s1):
    v0, rank0 = _top16(s0)
    v1, rank1 = _top16(s1)
    v0_all = jnp.concatenate(v0, axis=0)
    v1_all = jnp.concatenate(v1, axis=0)
    idx = lax.broadcasted_iota(jnp.int32, (PEER_TOPK, 1), 0)
    low = [jnp.where(idx < CAND_LOW_B[a], v0[a] + v1_all, -3.0e38) for a in range(len(CAND_LOW_B))]
    high = [jnp.where((idx >= len(CAND_LOW_B)) & (idx <= CAND_HIGH_A[b]), v0_all + v1[b], -3.0e38)
            for b in range(len(CAND_HIGH_A))]
    cand = jnp.concatenate(low + high, axis=0)
    cur = cand
    tau = None
    for r in range(PEER_TOPK):
        tau = jnp.max(cur, axis=0, keepdims=True)
        cur = jnp.where(cur == tau, -3.0e38, cur)
    top = v0[0] + v1[0]
    sel = jnp.where(cand >= tau, 1.0, 0.0)
    z = jnp.sum(sel * jnp.exp(cand - top), axis=0, keepdims=True)
    n_low = len(CAND_LOW_B)
    n_high = sel[n_low * PEER_TOPK:(n_low + 1) * PEER_TOPK, :]
    for b in range(1, len(CAND_HIGH_A)):
        n_high = n_high + sel[(n_low + b) * PEER_TOPK:(n_low + b + 1) * PEER_TOPK, :]
    cnt = jnp.zeros_like(s0)
    for a in range(PEER_TOPK):
        if a < n_low:
            n_a = jnp.sum(sel[a * PEER_TOPK:(a + 1) * PEER_TOPK, :], axis=0, keepdims=True)
        else:
            n_a = n_high[a:a + 1, :]
        cnt = jnp.where(rank0 == float(a), n_a, cnt)
    return cnt, jnp.exp(s0 - v0[0]) * (0.5 / z), rank1, jnp.exp(s1 - v1[0])


def _route_kernel(h_ref, wq_ref, keys_ref, n_ref, c0_ref, r1_ref, e1_ref, qt_ref, s_ref):
    qt_ref[...] = _dot_nt(wq_ref[...], h_ref[...]).astype(BF16)
    tt = h_ref.shape[0]

    def head(hh, _):
        base = pl.multiple_of(hh * PEER_QDIM, PEER_QDIM)
        s_ref[0] = _dot(keys_ref[2 * hh], qt_ref[pl.ds(base, PEER_HALF), :])
        s_ref[1] = _dot(keys_ref[2 * hh + 1], qt_ref[pl.ds(base + PEER_HALF, PEER_HALF), :])
        rows = pl.ds(pl.multiple_of(hh * PEER_NKEYS, PEER_NKEYS), PEER_NKEYS)
        half = pl.ds(pl.multiple_of(hh * (PEER_NKEYS // 2), PEER_NKEYS // 2), PEER_NKEYS // 2)
        parts = [_route_chunk(s_ref[0, :, lc * 128:(lc + 1) * 128],
                              s_ref[1, :, lc * 128:(lc + 1) * 128]) for lc in range(tt // 128)]
        cnt, c0, rank1, e1 = (jnp.concatenate(t, axis=1) for t in zip(*parts))
        n_ref[rows, :] = cnt
        c0_ref[rows, :] = c0
        r1_ref[half, :] = pltpu.bitcast(rank1.astype(BF16), jnp.uint32)
        e1_ref[half, :] = pltpu.bitcast(e1.astype(BF16), jnp.uint32)
        return 0

    lax.fori_loop(0, PEER_HEADS, head, 0)


def _route_call(h2, wq_t, keys):
    t_all = h2.shape[0]
    tt = TT_ROUTE
    rows = PEER_HEADS * PEER_NKEYS
    out = jax.ShapeDtypeStruct((rows, t_all), F32)
    out_b = jax.ShapeDtypeStruct((rows // 2, t_all), jnp.uint32)
    return pl.pallas_call(
        _route_kernel,
        out_shape=[out, out, out_b, out_b],
        grid=(t_all // tt,),
        in_specs=[pl.BlockSpec((tt, D_MODEL), lambda t: (t, 0)),
                  pl.BlockSpec(wq_t.shape, lambda t: (0, 0)),
                  pl.BlockSpec(keys.shape, lambda t: (0, 0, 0))],
        out_specs=[pl.BlockSpec((rows, tt), lambda t: (0, t))] * 2
        + [pl.BlockSpec((rows // 2, tt), lambda t: (0, t))] * 2,
        scratch_shapes=[pltpu.VMEM((PEER_HEADS * PEER_QDIM, tt), BF16),
                        pltpu.VMEM((2, PEER_NKEYS, tt), F32)],
        compiler_params=pltpu.CompilerParams(
            dimension_semantics=("parallel",), vmem_limit_bytes=VMEM_LIMIT),
        name="route",
    )(h2, wq_t, keys)


def _peer_kernel(h_ref, *refs):
    u_refs, vt_refs = refs[:PEER_DMA_SPLIT], refs[PEER_DMA_SPLIT:2 * PEER_DMA_SPLIT]
    (n_ref, c0_ref, r1_ref, e1_ref, x1_ref, mod_ref,
     o_ref, acc_ref, nsel_ref, csel_ref, a_ref, w_ref) = refs[2 * PEER_DMA_SPLIT:]
    e_idx = pl.program_id(1)
    te, tt = a_ref.shape
    blocks = te // PEER_NKEYS
    assert blocks == 8
    tile3 = (PEER_NKEYS // BF16_ROWS, BF16_ROWS, 128)

    @pl.when(e_idx == 0)
    def _():
        acc_ref[...] = jnp.zeros_like(acc_ref)

    for hh in range(PEER_HEADS):
        grp = pl.ds(pl.multiple_of(hh * PEER_NKEYS + e_idx * blocks, 8), 8)
        nsel_ref[hh * 8:(hh + 1) * 8, :] = n_ref[grp, :]
        csel_ref[hh * 8:(hh + 1) * 8, :] = c0_ref[grp, :]

    u_rows = te // PEER_DMA_SPLIT
    for k, u_ref in enumerate(u_refs):
        a_ref[k * u_rows:(k + 1) * u_rows, :] = _dot_nt(u_ref[...], h_ref[...])
    for ii in range(blocks):
        rows = slice(ii * PEER_NKEYS, (ii + 1) * PEER_NKEYS)
        for lc in range(tt // 128):
            lanes = slice(lc * 128, (lc + 1) * 128)
            p = None
            for hh in range(PEER_HEADS):
                n_row = nsel_ref[hh * 8 + ii:hh * 8 + ii + 1, lanes]
                c_row = csel_ref[hh * 8 + ii:hh * 8 + ii + 1, lanes]
                n_b = jnp.broadcast_to(n_row, (BF16_ROWS, 128)).astype(BF16)[None]
                c_b = jnp.broadcast_to(c_row, (BF16_ROWS, 128)).astype(BF16)[None]
                words = slice(hh * (PEER_NKEYS // 2), (hh + 1) * (PEER_NKEYS // 2))
                r1 = pltpu.bitcast(r1_ref[words, lanes], BF16).reshape(tile3)
                e1 = pltpu.bitcast(e1_ref[words, lanes], BF16).reshape(tile3)
                term = jnp.where(r1 < n_b, e1 * c_b, jnp.zeros((), BF16))
                p = term if p is None else p + term
            a = a_ref[rows, lanes]
            gelu2 = a * (1.0 + lax.erf(a * (2.0 ** -0.5)))
            w_ref[rows, lanes] = p.reshape(PEER_NKEYS, 128) * gelu2.astype(BF16)
    d_rows = D_MODEL // PEER_DMA_SPLIT
    for k, vt_ref in enumerate(vt_refs):
        acc_ref[k * d_rows:(k + 1) * d_rows, :] += _dot(vt_ref[0], w_ref[...])

    @pl.when(e_idx == pl.num_programs(1) - 1)
    def _():
        o_ref[...] = x1_ref[...] + mod_ref[0][5:6, :] * acc_ref[...].T


def _peer_call(h2, u_b, vt_b, n_t, c0_t, r1_t, e1_t, x1, mod3, tiles_per_batch):
    t_all = h2.shape[0]
    tt, te = TT_PEER, TE_PEER
    rows = PEER_HEADS * PEER_NKEYS

    split = PEER_DMA_SPLIT

    def tab(nrows=rows):
        return pl.BlockSpec((nrows, tt), lambda t, e: (0, t))

    def u_map(k, t, e):
        return (e * split + k, 0)

    def vt_map(k, t, e):
        return (e, k, 0)

    return pl.pallas_call(
        _peer_kernel,
        out_shape=jax.ShapeDtypeStruct((t_all, D_MODEL), F32),
        grid=(t_all // tt, PEER_EXPERTS // te),
        in_specs=[pl.BlockSpec((tt, D_MODEL), lambda t, e: (t, 0))]
        + [pl.BlockSpec((te // split, D_MODEL), functools.partial(u_map, k)) for k in range(split)]
        + [pl.BlockSpec((1, D_MODEL // split, te), functools.partial(vt_map, k)) for k in range(split)]
        + [tab(), tab(), tab(rows // 2), tab(rows // 2),
                  pl.BlockSpec((tt, D_MODEL), lambda t, e: (t, 0)),
                  pl.BlockSpec((1, N_MOD, D_MODEL), lambda t, e: (t // tiles_per_batch, 0, 0))],
        out_specs=pl.BlockSpec((tt, D_MODEL), lambda t, e: (t, 0)),
        scratch_shapes=[pltpu.VMEM((D_MODEL, tt), F32),
                        pltpu.VMEM((PEER_HEADS * 8, tt), F32), pltpu.VMEM((PEER_HEADS * 8, tt), F32),
                        pltpu.VMEM((te, tt), F32), pltpu.VMEM((te, tt), BF16)],
        compiler_params=pltpu.CompilerParams(
            dimension_semantics=("parallel", "arbitrary"), vmem_limit_bytes=VMEM_LIMIT),
        name="peer",
    )(h2, *([u_b] * split), *([vt_b] * split), n_t, c0_t, r1_t, e1_t, x1, mod3)


def _layer(x, c, positions, w_mod, b_mod, norm1_g, w_in, q_norm_swa, k_norm_swa, sinks,
           q_norm_fox, k_norm_fox, b_forget, w_out_swa, w_out_fox, w_o, norm2_g,
           peer_w_query, peer_sub_keys, peer_u, peer_v):
    bsz, seq, _ = x.shape
    split_at = [int(v) for v in np.cumsum(IN_SPLITS)[:-1]]
    w_qa, w_ka, w_va, w_qf, w_kf, w_vf, w_f, w_ga, w_gb = jnp.split(w_in, split_at, axis=1)
    w1 = jnp.concatenate(
        [_pad_heads_cols(w_qa, SWA_Q_HEADS), _pad_heads_cols(w_ka, SWA_KV_HEADS), w_va,
         _pad_heads_cols(w_qf, FOX_HEADS), _pad_heads_cols(w_kf, FOX_HEADS), w_vf,
         jnp.pad(w_f, ((0, 0), (0, HEAD_PAD - FOX_HEADS)))], axis=1).astype(BF16)
    wg = jnp.concatenate([w_ga, w_gb], axis=1).astype(BF16)
    bf_pad = jnp.pad(b_forget.astype(F32), (0, HEAD_PAD - FOX_HEADS)).reshape(1, HEAD_PAD)
    g1 = norm1_g.astype(F32).reshape(1, D_MODEL)
    g2 = norm2_g.astype(F32).reshape(1, D_MODEL)

    mod3 = _mod_call(c, w_mod, b_mod).reshape(bsz, N_MOD, D_MODEL)

    qa, ka, va, qf, kf, vf = _pre_call(
        x, mod3, g1, w1,
        _pad_head_gain(q_norm_swa, SWA_Q_HEADS), _pad_head_gain(k_norm_swa, SWA_KV_HEADS),
        _pad_head_gain(q_norm_fox, FOX_HEADS), _pad_head_gain(k_norm_fox, FOX_HEADS),
        bf_pad, positions)

    out_a = _swa_call(qa, ka, va, sinks)
    out_b = _fox_call(qf, kf, vf)

    x1, h2 = _post_call(x, mod3, g1, wg, out_a, out_b, w_out_swa.astype(BF16),
                        w_out_fox.astype(BF16), w_o.astype(BF16), g2)

    t_all = bsz * seq
    h2f = h2.reshape(t_all, D_MODEL)
    keys = peer_sub_keys.reshape(PEER_HEADS * 2, PEER_NKEYS, PEER_HALF).astype(BF16)
    n_t, c0_t, r1_t, e1_t = _route_call(h2f, peer_w_query.T.astype(BF16), keys)
    vt_tiles = peer_v.astype(BF16).reshape(PEER_EXPERTS // TE_PEER, TE_PEER, D_MODEL)
    vt_tiles = vt_tiles.transpose(0, 2, 1)
    out = _peer_call(h2f, peer_u.astype(BF16), vt_tiles, n_t, c0_t, r1_t, e1_t,
                     x1.reshape(t_all, D_MODEL), mod3, seq // TT_PEER)
    return out.reshape(bsz, seq, D_MODEL)


def kernel(x, c, positions, w_mod, b_mod, norm1_g, w_in, q_norm_swa, k_norm_swa, sinks, q_norm_fox, k_norm_fox, b_forget, w_out_swa, w_out_fox, w_o, norm2_g, peer_w_query, peer_sub_keys, peer_u, peer_v):
    for l in range(w_mod.shape[0]):
        x = _layer(x, c, positions, w_mod[l], b_mod[l], norm1_g[l], w_in[l], q_norm_swa[l],
                   k_norm_swa[l], sinks[l], q_norm_fox[l], k_norm_fox[l], b_forget[l],
                   w_out_swa[l], w_out_fox[l], w_o[l], norm2_g[l], peer_w_query[l],
                   peer_sub_keys[l], peer_u[l], peer_v[l])
    return x
```

```python
import functools

import jax
import jax.numpy as jnp
import numpy as np
from jax import lax
from jax.experimental import pallas as pl
from jax.experimental.pallas import tpu as pltpu

D_MODEL = 1024
HEAD_DIM = 64
HEAD_PAD = 128
SWA_Q_HEADS = 8
SWA_KV_HEADS = 2
SWA_GROUP = SWA_Q_HEADS // SWA_KV_HEADS
SWA_WINDOW = 128
FOX_HEADS = 8
BLOCK = 128
ROPE_THETA = 10000.0
PEER_HEADS = 8
PEER_NKEYS = 128
PEER_EXPERTS = PEER_NKEYS * PEER_NKEYS
PEER_QDIM = 256
PEER_HALF = PEER_QDIM // 2
PEER_TOPK = 16
N_MOD = 6
EPS = 1e-6
NEG_INF = -1e30

SWA_Q_W = SWA_Q_HEADS * HEAD_DIM
SWA_KV_W = SWA_KV_HEADS * HEAD_DIM
FOX_W = FOX_HEADS * HEAD_DIM
IN_SPLITS = (SWA_Q_W, SWA_KV_W, SWA_KV_W, FOX_W, FOX_W, FOX_W, FOX_HEADS, D_MODEL, D_MODEL)

C_QA = 0
C_KA = C_QA + SWA_Q_HEADS * HEAD_PAD
C_VA = C_KA + SWA_KV_HEADS * HEAD_PAD
C_F = C_VA + SWA_KV_W
C_QF = C_F + HEAD_PAD
C_KF = C_QF + FOX_HEADS * HEAD_PAD
C_VF = C_KF + FOX_HEADS * HEAD_PAD
C_END = C_VF + FOX_W
PROJ_W = 256

TM_PRE = 512
TQ_SWA = 512
TQ_FOX = 512
FOX_CHAIN_Q = 256
TT_ROUTE = 256
TT_PEER = 512
TE_PEER = 1024
PEER_DMA_SPLIT = 1
VMEM_LIMIT = 56 * 1024 * 1024

BF16 = jnp.bfloat16
F32 = jnp.float32
BF16_ROWS = 16


def _split3(v):
    hi = v.astype(BF16)
    r1 = v - hi.astype(F32)
    mid = r1.astype(BF16)
    lo = (r1 - mid.astype(F32)).astype(BF16)
    return hi, mid, lo


def _dot_nt(a, b):
    return lax.dot_general(a, b, (((1,), (1,)), ((), ())), preferred_element_type=F32)


def _dot(a, b):
    return jnp.dot(a, b, preferred_element_type=F32)


def _rms_modulate(x, g, scale, shift):
    y = x * lax.rsqrt(jnp.mean(x * x, axis=-1, keepdims=True) + EPS)
    return (y * g) * (1.0 + scale) + shift


def _mod_kernel(c_ref, w_ref, b_ref, o_ref):
    c = c_ref[...]
    a = c * jax.nn.sigmoid(c)
    a_hi, a_mid, a_lo = _split3(a)
    w = w_ref[...]
    w_hi, w_mid, w_lo = _split3(w)
    acc = _dot(a_hi, w_hi) + (_dot(a_hi, w_mid) + _dot(a_mid, w_hi))
    acc = acc + (_dot(a_mid, w_mid) + _dot(a_hi, w_lo) + _dot(a_lo, w_hi))
    o_ref[...] = acc + b_ref[...]


def _mod_call(c, w_mod, b_mod):
    bsz = c.shape[0]
    n = w_mod.shape[1]
    tn = 1024
    return pl.pallas_call(
        _mod_kernel,
        out_shape=jax.ShapeDtypeStruct((bsz, n), F32),
        grid=(n // tn,),
        in_specs=[
            pl.BlockSpec((bsz, D_MODEL), lambda j: (0, 0)),
            pl.BlockSpec((D_MODEL, tn), lambda j: (0, j)),
            pl.BlockSpec((1, tn), lambda j: (0, j)),
        ],
        out_specs=pl.BlockSpec((bsz, tn), lambda j: (0, j)),
        compiler_params=pltpu.CompilerParams(
            dimension_semantics=("parallel",), vmem_limit_bytes=VMEM_LIMIT),
        name="mod",
    )(c, w_mod, b_mod.reshape(1, n))


def _pre_kernel(x_ref, mod_ref, g1_ref, w_ref, gqa_ref, gka_ref, gqf_ref, gkf_ref, bf_ref,
                pos_ref, invf_ref, sign_ref, tri_ref, selq_ref, selk_ref, oneq_ref, onek_ref,
                qa_ref, ka_ref, va_ref, qf_ref, kf_ref, vf_ref, carry_ref):
    s_idx = pl.program_id(1)

    @pl.when(s_idx == 0)
    def _():
        carry_ref[...] = jnp.zeros_like(carry_ref)

    x = x_ref[0]
    mod = mod_ref[0]
    h = _rms_modulate(x, g1_ref[...], mod[1:2, :], mod[0:1, :]).astype(BF16)

    ang = pos_ref[0].astype(F32) * invf_ref[...]
    cos = jnp.cos(ang)
    sin_s = jnp.sin(ang) * sign_ref[...]
    lane = lax.broadcasted_iota(jnp.int32, (1, HEAD_PAD), 1)
    low_half = lane < HEAD_DIM // 2

    def head_norm(c, g):
        return c * lax.rsqrt(jnp.sum(c * c, axis=-1, keepdims=True) * (1.0 / HEAD_DIM) + EPS) * g

    def rope(c):
        partner = jnp.where(low_half, pltpu.roll(c, HEAD_PAD - HEAD_DIM // 2, 1),
                            pltpu.roll(c, HEAD_DIM // 2, 1))
        return c * cos + partner * sin_s

    scale = HEAD_DIM ** -0.5

    def proj_chunks(c0, width):
        for j in range(width // PROJ_W):
            wide = _dot(h, w_ref[:, c0 + j * PROJ_W:c0 + (j + 1) * PROJ_W])
            for i in range(PROJ_W // HEAD_PAD):
                yield wide[:, i * HEAD_PAD:(i + 1) * HEAD_PAD]

    for hh, c in enumerate(proj_chunks(C_QA, SWA_Q_HEADS * HEAD_PAD)):
        c = rope(head_norm(c, gqa_ref[:, hh * HEAD_PAD:(hh + 1) * HEAD_PAD])) * scale
        qa_ref[0, :, hh * HEAD_PAD:(hh + 1) * HEAD_PAD] = c.astype(BF16)
    for hh, c in enumerate(proj_chunks(C_KA, SWA_KV_HEADS * HEAD_PAD)):
        c = rope(head_norm(c, gka_ref[:, hh * HEAD_PAD:(hh + 1) * HEAD_PAD]))
        ka_ref[0, :, hh * HEAD_PAD:(hh + 1) * HEAD_PAD] = c.astype(BF16)
    va, f_logit = proj_chunks(C_VA, SWA_KV_W + HEAD_PAD)
    va_ref[0] = va.astype(BF16)
    for j, c in enumerate(proj_chunks(C_VF, FOX_W)):
        vf_ref[0, 0, j * HEAD_PAD:(j + 1) * HEAD_PAD, :] = c.T.astype(BF16)

    f_logit = f_logit + bf_ref[...]
    log_f = jnp.minimum(f_logit, 0.0) - jnp.log1p(jnp.exp(-jnp.abs(f_logit)))
    l_hi, l_mid, l_lo = _split3(log_f)
    tri = tri_ref[...]
    dcum = (_dot(tri, l_hi) + _dot(tri, l_mid) + _dot(tri, l_lo)) + carry_ref[...]
    tm = dcum.shape[0]
    carry_ref[...] = dcum[tm - 1:tm, :]

    d_hi, d_mid, d_lo = _split3(dcum)
    d3 = jnp.concatenate([d_hi, d_mid, d_lo], axis=1)
    aug_q = _dot(d3, selq_ref[...]) + oneq_ref[...]
    aug_k = onek_ref[...] - _dot(d3, selk_ref[...])

    for hh, c in enumerate(proj_chunks(C_QF, FOX_HEADS * HEAD_PAD)):
        sl = slice(hh * HEAD_PAD, (hh + 1) * HEAD_PAD)
        qf_ref[0, :, sl] = (head_norm(c, gqf_ref[:, sl]) * scale + aug_q[:, sl]).astype(BF16)
    for hh, c in enumerate(proj_chunks(C_KF, FOX_HEADS * HEAD_PAD)):
        sl = slice(hh * HEAD_PAD, (hh + 1) * HEAD_PAD)
        kf_ref[0, :, sl] = (head_norm(c, gkf_ref[:, sl]) + aug_k[:, sl]).astype(BF16)


def _pad_heads_cols(w, nheads):
    k = w.shape[0]
    w = w.reshape(k, nheads, HEAD_DIM)
    w = jnp.pad(w, ((0, 0), (0, 0), (0, HEAD_PAD - HEAD_DIM)))
    return w.reshape(k, nheads * HEAD_PAD)


def _pad_head_gain(g, nheads):
    g = jnp.pad(g.astype(F32), (0, HEAD_PAD - HEAD_DIM))
    return jnp.tile(g, nheads).reshape(1, nheads * HEAD_PAD)


def _decay_selectors():
    selq = np.zeros((3 * HEAD_PAD, FOX_HEADS * HEAD_PAD), np.float32)
    selk = np.zeros((3 * HEAD_PAD, FOX_HEADS * HEAD_PAD), np.float32)
    oneq = np.zeros((1, FOX_HEADS * HEAD_PAD), np.float32)
    onek = np.zeros((1, FOX_HEADS * HEAD_PAD), np.float32)
    for hh in range(FOX_HEADS):
        for p in range(3):
            selq[p * HEAD_PAD + hh, hh * HEAD_PAD + HEAD_DIM + p] = 1.0
            selk[p * HEAD_PAD + hh, hh * HEAD_PAD + HEAD_DIM + 3 + p] = 1.0
            oneq[0, hh * HEAD_PAD + HEAD_DIM + 3 + p] = 1.0
            onek[0, hh * HEAD_PAD + HEAD_DIM + p] = 1.0
    return (jnp.asarray(selq, BF16), jnp.asarray(selk, BF16), jnp.asarray(oneq), jnp.asarray(onek))


def _pre_call(x, mod3, g1, w1, gqa, gka, gqf, gkf, bf_pad, positions):
    bsz, seq, _ = x.shape
    tm = TM_PRE
    half = HEAD_DIM // 2
    inv_freq = ROPE_THETA ** (-jnp.arange(half, dtype=F32) / half)
    invf = jnp.concatenate([inv_freq, inv_freq, jnp.zeros((HEAD_PAD - HEAD_DIM,), F32)]).reshape(1, HEAD_PAD)
    sign = jnp.concatenate([-jnp.ones((half,), F32), jnp.ones((half,), F32),
                            jnp.zeros((HEAD_PAD - HEAD_DIM,), F32)]).reshape(1, HEAD_PAD)
    tri = jnp.asarray(np.tril(np.ones((tm, tm), np.float32)), BF16)
    selq, selk, oneq, onek = _decay_selectors()

    def full(a):
        return pl.BlockSpec(a.shape, lambda b, s: (0,) * a.ndim)

    def tok(width):
        return pl.BlockSpec((1, tm, width), lambda b, s: (b, s, 0))

    consts = (g1, w1, gqa, gka, gqf, gkf, bf_pad)
    tail = (invf, sign, tri, selq, selk, oneq, onek)
    out_w = (SWA_Q_HEADS * HEAD_PAD, SWA_KV_HEADS * HEAD_PAD, SWA_KV_W,
             FOX_HEADS * HEAD_PAD, FOX_HEADS * HEAD_PAD)
    assert tm == TQ_FOX
    return pl.pallas_call(
        _pre_kernel,
        out_shape=[jax.ShapeDtypeStruct((bsz, seq, w), BF16) for w in out_w]
        + [jax.ShapeDtypeStruct((bsz, seq // tm, FOX_W, tm), BF16)],
        grid=(bsz, seq // tm),
        in_specs=[tok(D_MODEL), pl.BlockSpec((1, N_MOD, D_MODEL), lambda b, s: (b, 0, 0))]
        + [full(a) for a in consts] + [tok(1)] + [full(a) for a in tail],
        out_specs=[tok(w) for w in out_w]
        + [pl.BlockSpec((1, 1, FOX_W, tm), lambda b, s: (b, s, 0, 0))],
        scratch_shapes=[pltpu.VMEM((1, HEAD_PAD), F32)],
        compiler_params=pltpu.CompilerParams(
            dimension_semantics=("parallel", "arbitrary"), vmem_limit_bytes=VMEM_LIMIT),
        name="pre",
    )(x, mod3, *consts, positions.reshape(bsz, seq, 1), *tail)


def _swa_kernel(sink_ref, q_ref, kc_ref, kp_ref, vc_ref, vp_ref, o_ref):
    t_idx = pl.program_id(1)
    tq = q_ref.shape[1]
    nsub = tq // BLOCK
    k_all = jnp.concatenate([kp_ref[0], kc_ref[0]], axis=0)
    v_all = jnp.concatenate([vp_ref[0], vc_ref[0]], axis=0)
    v_swap = jnp.concatenate([v_all[:, HEAD_DIM:], v_all[:, :HEAD_DIM]], axis=1)

    rows = SWA_GROUP * BLOCK
    r = lax.broadcasted_iota(jnp.int32, (rows, 2 * BLOCK), 0)
    kcol = lax.broadcasted_iota(jnp.int32, (rows, 2 * BLOCK), 1)
    rel = (r % BLOCK) + BLOCK - kcol
    in_window = (rel >= 0) & (rel < SWA_WINDOW)
    rgrp = lax.broadcasted_iota(jnp.int32, (rows, 1), 0) // BLOCK
    lane_lo = lax.broadcasted_iota(jnp.int32, (1, HEAD_PAD), 1) < HEAD_DIM

    for j in range(nsub):
        lo_key = jnp.where(t_idx == 0, BLOCK, 0) if j == 0 else 0
        mask = in_window & (kcol >= lo_key)
        vb = v_all[j * BLOCK:(j + 2) * BLOCK, :]
        vsb = v_swap[j * BLOCK:(j + 2) * BLOCK, :]
        for g in range(SWA_KV_HEADS):
            kb = k_all[j * BLOCK:(j + 2) * BLOCK, g * HEAD_PAD:(g + 1) * HEAD_PAD]
            qg = jnp.concatenate(
                [q_ref[0, j * BLOCK:(j + 1) * BLOCK,
                       (g * SWA_GROUP + u) * HEAD_PAD:(g * SWA_GROUP + u + 1) * HEAD_PAD]
                 for u in range(SWA_GROUP)], axis=0)
            s = jnp.where(mask, _dot_nt(qg, kb), NEG_INF)
            sink = jnp.zeros((rows, 1), F32)
            for u in range(SWA_GROUP):
                sink = jnp.where(rgrp == u, sink_ref[g * SWA_GROUP + u], sink)
            m = jnp.maximum(jnp.max(s, axis=-1, keepdims=True), sink)
            p = jnp.exp(s - m)
            denom = jnp.sum(p, axis=-1, keepdims=True) + jnp.exp(sink - m)
            pb = p.astype(BF16)
            inv = 1.0 / denom
            o_nat = _dot(pb, vb) * inv
            o_swp = _dot(pb, vsb) * inv
            o_lo, o_hi = (o_nat, o_swp) if g == 0 else (o_swp, o_nat)
            for w in range(SWA_GROUP // 2):
                ev = o_lo[(2 * w) * BLOCK:(2 * w + 1) * BLOCK, :]
                od = o_hi[(2 * w + 1) * BLOCK:(2 * w + 2) * BLOCK, :]
                pc = g * (SWA_GROUP // 2) + w
                o_ref[0, j * BLOCK:(j + 1) * BLOCK, pc * HEAD_PAD:(pc + 1) * HEAD_PAD] = (
                    jnp.where(lane_lo, ev, od).astype(BF16))


def _swa_call(qa, ka, va, sinks):
    bsz, seq, _ = qa.shape
    tq = TQ_SWA
    per = tq // BLOCK

    def cur(width):
        return pl.BlockSpec((1, tq, width), lambda b, t: (b, t, 0))

    def prev(width):
        return pl.BlockSpec((1, BLOCK, width), lambda b, t: (b, jnp.maximum(t * per - 1, 0), 0))

    return pl.pallas_call(
        _swa_kernel,
        out_shape=jax.ShapeDtypeStruct((bsz, seq, SWA_Q_W), BF16),
        grid=(bsz, seq // tq),
        in_specs=[pl.BlockSpec(memory_space=pltpu.SMEM),
                  cur(SWA_Q_HEADS * HEAD_PAD),
                  cur(SWA_KV_HEADS * HEAD_PAD), prev(SWA_KV_HEADS * HEAD_PAD),
                  cur(SWA_KV_W), prev(SWA_KV_W)],
        out_specs=cur(SWA_Q_W),
        compiler_params=pltpu.CompilerParams(
            dimension_semantics=("parallel", "parallel"), vmem_limit_bytes=VMEM_LIMIT),
        name="swa",
    )(sinks.astype(F32), qa, ka, ka, va, va)


def _fox_kernel(q_ref, k_ref, vt_ref, o_ref):
    qi = pl.program_id(2)
    tq = q_ref.shape[1]
    tk = tq
    nq = tq // FOX_CHAIN_Q
    key = lax.broadcasted_iota(jnp.int32, (tk, FOX_CHAIN_Q), 0)
    qry = lax.broadcasted_iota(jnp.int32, (tk, FOX_CHAIN_Q), 1)

    def step(j, carry, masked):
        start = pl.multiple_of(j * tk, tk)
        chains = [(e, c) for e in range(2) for c in range(nq)]

        def scores(e, c):
            k = k_ref[0, pl.ds(start, tk), e * HEAD_PAD:(e + 1) * HEAD_PAD]
            q = q_ref[0, c * FOX_CHAIN_Q:(c + 1) * FOX_CHAIN_Q, e * HEAD_PAD:(e + 1) * HEAD_PAD]
            s = _dot_nt(k, q)
            if masked:
                s = jnp.where(key <= qry + c * FOX_CHAIN_Q, s, NEG_INF)
            return s

        out = []
        s_next = scores(*chains[0])
        for idx, (e, c) in enumerate(chains):
            s = s_next
            if idx + 1 < len(chains):
                s_next = scores(*chains[idx + 1])
            m, l, acc = carry[idx]
            m_new = jnp.maximum(m, jnp.max(s, axis=0, keepdims=True))
            alpha = jnp.exp(m - m_new)
            p = jnp.exp(s - m_new)
            l = alpha * l + jnp.sum(p, axis=0, keepdims=True)
            vt = vt_ref[0, j, e * HEAD_DIM:(e + 1) * HEAD_DIM, :]
            acc = alpha * acc + _dot(vt, p.astype(BF16))
            out.append((m_new, l, acc))
        return tuple(out)

    init1 = (jnp.full((1, FOX_CHAIN_Q), NEG_INF, F32), jnp.zeros((1, FOX_CHAIN_Q), F32),
             jnp.zeros((HEAD_DIM, FOX_CHAIN_Q), F32))
    carry = lax.fori_loop(0, qi, functools.partial(step, masked=False), (init1,) * (2 * nq))
    fin = step(qi, carry, True)
    o_t = jnp.concatenate(
        [jnp.concatenate([fin[e * nq + c][2] * (1.0 / fin[e * nq + c][1]) for c in range(nq)], axis=1)
         for e in range(2)], axis=0)
    o_ref[0] = o_t.T.astype(BF16)


def _fox_call(qf, kf, vft):
    bsz, seq, _ = qf.shape
    tq = TQ_FOX
    pairs = FOX_HEADS // 2
    return pl.pallas_call(
        _fox_kernel,
        out_shape=jax.ShapeDtypeStruct((bsz, seq, FOX_W), BF16),
        grid=(bsz, pairs, seq // tq),
        in_specs=[pl.BlockSpec((1, tq, 2 * HEAD_PAD), lambda b, p, t: (b, t, p)),
                  pl.BlockSpec((1, seq, 2 * HEAD_PAD), lambda b, p, t: (b, 0, p)),
                  pl.BlockSpec((1, seq // tq, 2 * HEAD_DIM, tq), lambda b, p, t: (b, 0, p, 0))],
        out_specs=pl.BlockSpec((1, tq, 2 * HEAD_DIM), lambda b, p, t: (b, t, p)),
        compiler_params=pltpu.CompilerParams(
            dimension_semantics=("parallel", "parallel", "arbitrary"),
            vmem_limit_bytes=VMEM_LIMIT),
        name="fox",
    )(qf, kf, vft)


def _post_kernel(x_ref, mod_ref, g1_ref, wg_ref, oa_ref, ob_ref, wa_ref, wb_ref, wo_ref, g2_ref,
                 x1_ref, h2_ref):
    x = x_ref[0]
    mod = mod_ref[0]
    h = _rms_modulate(x, g1_ref[...], mod[1:2, :], mod[0:1, :]).astype(BF16)
    gate_a = jax.nn.sigmoid(_dot(h, wg_ref[:, :D_MODEL]))
    gate_b = jax.nn.sigmoid(_dot(h, wg_ref[:, D_MODEL:]))
    merged = gate_a * _dot(oa_ref[0], wa_ref[...]) + gate_b * _dot(ob_ref[0], wb_ref[...])
    x1 = x + mod[2:3, :] * _dot(merged.astype(BF16), wo_ref[...])
    x1_ref[0] = x1
    h2_ref[0] = _rms_modulate(x1, g2_ref[...], mod[4:5, :], mod[3:4, :]).astype(BF16)


def _post_call(x, mod3, g1, wg, out_a, out_b, wa, wb, wo, g2):
    bsz, seq, _ = x.shape
    tm = TM_PRE

    def full(a):
        return pl.BlockSpec(a.shape, lambda b, s: (0,) * a.ndim)

    def tok(width):
        return pl.BlockSpec((1, tm, width), lambda b, s: (b, s, 0))

    return pl.pallas_call(
        _post_kernel,
        out_shape=[jax.ShapeDtypeStruct((bsz, seq, D_MODEL), F32),
                   jax.ShapeDtypeStruct((bsz, seq, D_MODEL), BF16)],
        grid=(bsz, seq // tm),
        in_specs=[tok(D_MODEL), pl.BlockSpec((1, N_MOD, D_MODEL), lambda b, s: (b, 0, 0)),
                  full(g1), full(wg), tok(SWA_Q_W), tok(FOX_W), full(wa), full(wb), full(wo),
                  full(g2)],
        out_specs=[tok(D_MODEL), tok(D_MODEL)],
        compiler_params=pltpu.CompilerParams(
            dimension_semantics=("parallel", "parallel"), vmem_limit_bytes=VMEM_LIMIT),
        name="post",
    )(x, mod3, g1, wg, out_a, out_b, wa, wb, wo, g2)


def _top16(s):
    cur = s
    rank = jnp.full(s.shape, float(PEER_TOPK), F32)
    vals = []
    for r in range(PEER_TOPK):
        mx = jnp.max(cur, axis=0, keepdims=True)
        vals.append(mx)
        hit = cur == mx
        rank = jnp.where(hit, float(r), rank)
        cur = jnp.where(hit, -3.0e38, cur)
    return vals, rank


CAND_LOW_B = (16, 8, 5, 4)
CAND_HIGH_A = (15, 7, 4)


def _route_chunk(s0, s1):
    v0, rank0 = _top16(s0)
    v1, rank1 = _top16(s1)
    v0_all = jnp.concatenate(v0, axis=0)
    v1_all = jnp.concatenate(v1, axis=0)
    idx = lax.broadcasted_iota(jnp.int32, (PEER_TOPK, 1), 0)
    low = [jnp.where(idx < CAND_LOW_B[a], v0[a] + v1_all, -3.0e38) for a in range(len(CAND_LOW_B))]
    high = [jnp.where((idx >= len(CAND_LOW_B)) & (idx <= CAND_HIGH_A[b]), v0_all + v1[b], -3.0e38)
            for b in range(len(CAND_HIGH_A))]
    cand = jnp.concatenate(low + high, axis=0)
    cur = cand
    tau = None
    for r in range(PEER_TOPK):
        tau = jnp.max(cur, axis=0, keepdims=True)
        cur = jnp.where(cur == tau, -3.0e38, cur)
    top = v0[0] + v1[0]
    sel = jnp.where(cand >= tau, 1.0, 0.0)
    z = jnp.sum(sel * jnp.exp(cand - top), axis=0, keepdims=True)
    n_low = len(CAND_LOW_B)
    n_high = sel[n_low * PEER_TOPK:(n_low + 1) * PEER_TOPK, :]
    for b in range(1, len(CAND_HIGH_A)):
        n_high = n_high + sel[(n_low + b) * PEER_TOPK:(n_low + b + 1) * PEER_TOPK, :]
    cnt = jnp.zeros_like(s0)
    for a in range(PEER_TOPK):
        if a < n_low:
            n_a = jnp.sum(sel[a * PEER_TOPK:(a + 1) * PEER_TOPK, :], axis=0, keepdims=True)
        else:
            n_a = n_high[a:a + 1, :]
        cnt = jnp.where(rank0 == float(a), n_a, cnt)
    return cnt, jnp.exp(s0 - v0[0]) * (0.5 / z), rank1, jnp.exp(s1 - v1[0])


def _route_kernel(h_ref, wq_ref, keys_ref, n_ref, c0_ref, r1_ref, e1_ref, qt_ref, s_ref):
    qt_ref[...] = _dot_nt(wq_ref[...], h_ref[...]).astype(BF16)
    tt = h_ref.shape[0]

    def head(hh, _):
        base = pl.multiple_of(hh * PEER_QDIM, PEER_QDIM)
        s_ref[0] = _dot(keys_ref[2 * hh], qt_ref[pl.ds(base, PEER_HALF), :])
        s_ref[1] = _dot(keys_ref[2 * hh + 1], qt_ref[pl.ds(base + PEER_HALF, PEER_HALF), :])
        rows = pl.ds(pl.multiple_of(hh * PEER_NKEYS, PEER_NKEYS), PEER_NKEYS)
        half = pl.ds(pl.multiple_of(hh * (PEER_NKEYS // 2), PEER_NKEYS // 2), PEER_NKEYS // 2)
        parts = [_route_chunk(s_ref[0, :, lc * 128:(lc + 1) * 128],
                              s_ref[1, :, lc * 128:(lc + 1) * 128]) for lc in range(tt // 128)]
        cnt, c0, rank1, e1 = (jnp.concatenate(t, axis=1) for t in zip(*parts))
        n_ref[rows, :] = cnt
        c0_ref[rows, :] = c0
        r1_ref[half, :] = pltpu.bitcast(rank1.astype(BF16), jnp.uint32)
        e1_ref[half, :] = pltpu.bitcast(e1.astype(BF16), jnp.uint32)
        return 0

    lax.fori_loop(0, PEER_HEADS, head, 0)


def _route_call(h2, wq_t, keys):
    t_all = h2.shape[0]
    tt = TT_ROUTE
    rows = PEER_HEADS * PEER_NKEYS
    out = jax.ShapeDtypeStruct((rows, t_all), F32)
    out_b = jax.ShapeDtypeStruct((rows // 2, t_all), jnp.uint32)
    return pl.pallas_call(
        _route_kernel,
        out_shape=[out, out, out_b, out_b],
        grid=(t_all // tt,),
        in_specs=[pl.BlockSpec((tt, D_MODEL), lambda t: (t, 0)),
                  pl.BlockSpec(wq_t.shape, lambda t: (0, 0)),
                  pl.BlockSpec(keys.shape, lambda t: (0, 0, 0))],
        out_specs=[pl.BlockSpec((rows, tt), lambda t: (0, t))] * 2
        + [pl.BlockSpec((rows // 2, tt), lambda t: (0, t))] * 2,
        scratch_shapes=[pltpu.VMEM((PEER_HEADS * PEER_QDIM, tt), BF16),
                        pltpu.VMEM((2, PEER_NKEYS, tt), F32)],
        compiler_params=pltpu.CompilerParams(
            dimension_semantics=("parallel",), vmem_limit_bytes=VMEM_LIMIT),
        name="route",
    )(h2, wq_t, keys)


def _peer_kernel(h_ref, *refs):
    u_refs, vt_refs = refs[:PEER_DMA_SPLIT], refs[PEER_DMA_SPLIT:2 * PEER_DMA_SPLIT]
    (n_ref, c0_ref, r1_ref, e1_ref, x1_ref, mod_ref,
     o_ref, acc_ref, nsel_ref, csel_ref, a_ref, w_ref) = refs[2 * PEER_DMA_SPLIT:]
    e_idx = pl.program_id(1)
    te, tt = a_ref.shape
    blocks = te // PEER_NKEYS
    assert blocks == 8
    tile3 = (PEER_NKEYS // BF16_ROWS, BF16_ROWS, 128)

    @pl.when(e_idx == 0)
    def _():
        acc_ref[...] = jnp.zeros_like(acc_ref)

    for hh in range(PEER_HEADS):
        grp = pl.ds(pl.multiple_of(hh * PEER_NKEYS + e_idx * blocks, 8), 8)
        nsel_ref[hh * 8:(hh + 1) * 8, :] = n_ref[grp, :]
        csel_ref[hh * 8:(hh + 1) * 8, :] = c0_ref[grp, :]

    u_rows = te // PEER_DMA_SPLIT
    for k, u_ref in enumerate(u_refs):
        a_ref[k * u_rows:(k + 1) * u_rows, :] = _dot_nt(u_ref[...], h_ref[...])
    for ii in range(blocks):
        rows = slice(ii * PEER_NKEYS, (ii + 1) * PEER_NKEYS)
        for lc in range(tt // 128):
            lanes = slice(lc * 128, (lc + 1) * 128)
            p = None
            for hh in range(PEER_HEADS):
                n_row = nsel_ref[hh * 8 + ii:hh * 8 + ii + 1, lanes]
                c_row = csel_ref[hh * 8 + ii:hh * 8 + ii + 1, lanes]
                n_b = jnp.broadcast_to(n_row, (BF16_ROWS, 128)).astype(BF16)[None]
                c_b = jnp.broadcast_to(c_row, (BF16_ROWS, 128)).astype(BF16)[None]
                words = slice(hh * (PEER_NKEYS // 2), (hh + 1) * (PEER_NKEYS // 2))
                r1 = pltpu.bitcast(r1_ref[words, lanes], BF16).reshape(tile3)
                e1 = pltpu.bitcast(e1_ref[words, lanes], BF16).reshape(tile3)
                term = jnp.where(r1 < n_b, e1 * c_b, jnp.zeros((), BF16))
                p = term if p is None else p + term
            a = a_ref[rows, lanes]
            gelu2 = a * (1.0 + lax.erf(a * (2.0 ** -0.5)))
            w_ref[rows, lanes] = p.reshape(PEER_NKEYS, 128) * gelu2.astype(BF16)
    d_rows = D_MODEL // PEER_DMA_SPLIT
    for k, vt_ref in enumerate(vt_refs):
        acc_ref[k * d_rows:(k + 1) * d_rows, :] += _dot(vt_ref[0], w_ref[...])

    @pl.when(e_idx == pl.num_programs(1) - 1)
    def _():
        o_ref[...] = x1_ref[...] + mod_ref[0][5:6, :] * acc_ref[...].T


def _peer_call(h2, u_b, vt_b, n_t, c0_t, r1_t, e1_t, x1, mod3, tiles_per_batch):
    t_all = h2.shape[0]
    tt, te = TT_PEER, TE_PEER
    rows = PEER_HEADS * PEER_NKEYS

    split = PEER_DMA_SPLIT

    def tab(nrows=rows):
        return pl.BlockSpec((nrows, tt), lambda t, e: (0, t))

    def u_map(k, t, e):
        return (e * split + k, 0)

    def vt_map(k, t, e):
        return (e, k, 0)

    return pl.pallas_call(
        _peer_kernel,
        out_shape=jax.ShapeDtypeStruct((t_all, D_MODEL), F32),
        grid=(t_all // tt, PEER_EXPERTS // te),
        in_specs=[pl.BlockSpec((tt, D_MODEL), lambda t, e: (t, 0))]
        + [pl.BlockSpec((te // split, D_MODEL), functools.partial(u_map, k)) for k in range(split)]
        + [pl.BlockSpec((1, D_MODEL // split, te), functools.partial(vt_map, k)) for k in range(split)]
        + [tab(), tab(), tab(rows // 2), tab(rows // 2),
                  pl.BlockSpec((tt, D_MODEL), lambda t, e: (t, 0)),
                  pl.BlockSpec((1, N_MOD, D_MODEL), lambda t, e: (t // tiles_per_batch, 0, 0))],
        out_specs=pl.BlockSpec((tt, D_MODEL), lambda t, e: (t, 0)),
        scratch_shapes=[pltpu.VMEM((D_MODEL, tt), F32),
                        pltpu.VMEM((PEER_HEADS * 8, tt), F32), pltpu.VMEM((PEER_HEADS * 8, tt), F32),
                        pltpu.VMEM((te, tt), F32), pltpu.VMEM((te, tt), BF16)],
        compiler_params=pltpu.CompilerParams(
            dimension_semantics=("parallel", "arbitrary"), vmem_limit_bytes=VMEM_LIMIT),
        name="peer",
    )(h2, *([u_b] * split), *([vt_b] * split), n_t, c0_t, r1_t, e1_t, x1, mod3)


def _layer(x, c, positions, w_mod, b_mod, norm1_g, w_in, q_norm_swa, k_norm_swa, sinks,
           q_norm_fox, k_norm_fox, b_forget, w_out_swa, w_out_fox, w_o, norm2_g,
           peer_w_query, peer_sub_keys, peer_u, peer_v):
    bsz, seq, _ = x.shape
    split_at = [int(v) for v in np.cumsum(IN_SPLITS)[:-1]]
    w_qa, w_ka, w_va, w_qf, w_kf, w_vf, w_f, w_ga, w_gb = jnp.split(w_in, split_at, axis=1)
    w1 = jnp.concatenate(
        [_pad_heads_cols(w_qa, SWA_Q_HEADS), _pad_heads_cols(w_ka, SWA_KV_HEADS), w_va,
         jnp.pad(w_f, ((0, 0), (0, HEAD_PAD - FOX_HEADS))),
         _pad_heads_cols(w_qf, FOX_HEADS), _pad_heads_cols(w_kf, FOX_HEADS), w_vf],
        axis=1).astype(BF16)
    wg = jnp.concatenate([w_ga, w_gb], axis=1).astype(BF16)
    bf_pad = jnp.pad(b_forget.astype(F32), (0, HEAD_PAD - FOX_HEADS)).reshape(1, HEAD_PAD)
    g1 = norm1_g.astype(F32).reshape(1, D_MODEL)
    g2 = norm2_g.astype(F32).reshape(1, D_MODEL)

    mod3 = _mod_call(c, w_mod, b_mod).reshape(bsz, N_MOD, D_MODEL)

    qa, ka, va, qf, kf, vf = _pre_call(
        x, mod3, g1, w1,
        _pad_head_gain(q_norm_swa, SWA_Q_HEADS), _pad_head_gain(k_norm_swa, SWA_KV_HEADS),
        _pad_head_gain(q_norm_fox, FOX_HEADS), _pad_head_gain(k_norm_fox, FOX_HEADS),
        bf_pad, positions)

    out_a = _swa_call(qa, ka, va, sinks)
    out_b = _fox_call(qf, kf, vf)

    x1, h2 = _post_call(x, mod3, g1, wg, out_a, out_b, w_out_swa.astype(BF16),
                        w_out_fox.astype(BF16), w_o.astype(BF16), g2)

    t_all = bsz * seq
    h2f = h2.reshape(t_all, D_MODEL)
    keys = peer_sub_keys.reshape(PEER_HEADS * 2, PEER_NKEYS, PEER_HALF).astype(BF16)
    n_t, c0_t, r1_t, e1_t = _route_call(h2f, peer_w_query.T.astype(BF16), keys)
    vt_tiles = peer_v.astype(BF16).reshape(PEER_EXPERTS // TE_PEER, TE_PEER, D_MODEL)
    vt_tiles = vt_tiles.transpose(0, 2, 1)
    out = _peer_call(h2f, peer_u.astype(BF16), vt_tiles, n_t, c0_t, r1_t, e1_t,
                     x1.reshape(t_all, D_MODEL), mod3, seq // TT_PEER)
    return out.reshape(bsz, seq, D_MODEL)


def kernel(x, c, positions, w_mod, b_mod, norm1_g, w_in, q_norm_swa, k_norm_swa, sinks, q_norm_fox, k_norm_fox, b_forget, w_out_swa, w_out_fox, w_o, norm2_g, peer_w_query, peer_sub_keys, peer_u, peer_v):
    for l in range(w_mod.shape[0]):
        x = _layer(x, c, positions, w_mod[l], b_mod[l], norm1_g[l], w_in[l], q_norm_swa[l],
                   k_norm_swa[l], sinks[l], q_norm_fox[l], k_norm_fox[l], b_forget[l],
                   w_out_swa[l], w_out_fox[l], w_o[l], norm2_g[l], peer_w_query[l],
                   peer_sub_keys[l], peer_u[l], peer_v[l])
    return x
```

```python
import functools

import jax
import jax.numpy as jnp
import numpy as np
from jax import lax
from jax.experimental import pallas as pl
from jax.experimental.pallas import tpu as pltpu

D_MODEL = 1024
HEAD_DIM = 64
HEAD_PAD = 128
SWA_Q_HEADS = 8
SWA_KV_HEADS = 2
SWA_GROUP = SWA_Q_HEADS // SWA_KV_HEADS
SWA_WINDOW = 128
FOX_HEADS = 8
BLOCK = 128
ROPE_THETA = 10000.0
PEER_HEADS = 8
PEER_NKEYS = 128
PEER_EXPERTS = PEER_NKEYS * PEER_NKEYS
PEER_QDIM = 256
PEER_HALF = PEER_QDIM // 2
PEER_TOPK = 16
N_MOD = 6
EPS = 1e-6
NEG_INF = -1e30

SWA_Q_W = SWA_Q_HEADS * HEAD_DIM
SWA_KV_W = SWA_KV_HEADS * HEAD_DIM
FOX_W = FOX_HEADS * HEAD_DIM
IN_SPLITS = (SWA_Q_W, SWA_KV_W, SWA_KV_W, FOX_W, FOX_W, FOX_W, FOX_HEADS, D_MODEL, D_MODEL)

C_QA = 0
C_KA = C_QA + SWA_Q_HEADS * HEAD_PAD
C_VA = C_KA + SWA_KV_HEADS * HEAD_PAD
C_F = C_VA + SWA_KV_W
C_QF = C_F + HEAD_PAD
C_KF = C_QF + FOX_HEADS * HEAD_PAD
C_VF = C_KF + FOX_HEADS * HEAD_PAD
C_END = C_VF + FOX_W
PROJ_W = 256

TM_PRE = 512
TQ_SWA = 512
TQ_FOX = 512
FOX_CHAIN_Q = 256
TT_ROUTE = 256
TT_PEER = 512
TE_PEER = 1024
PEER_AHEAD = 2
VMEM_LIMIT = 56 * 1024 * 1024

BF16 = jnp.bfloat16
F32 = jnp.float32
BF16_ROWS = 16


def _split3(v):
    hi = v.astype(BF16)
    r1 = v - hi.astype(F32)
    mid = r1.astype(BF16)
    lo = (r1 - mid.astype(F32)).astype(BF16)
    return hi, mid, lo


def _dot_nt(a, b):
    return lax.dot_general(a, b, (((1,), (1,)), ((), ())), preferred_element_type=F32)


def _dot(a, b):
    return jnp.dot(a, b, preferred_element_type=F32)


def _rms_modulate(x, g, scale, shift):
    y = x * lax.rsqrt(jnp.mean(x * x, axis=-1, keepdims=True) + EPS)
    return (y * g) * (1.0 + scale) + shift


def _mod_kernel(c_ref, w_ref, b_ref, o_ref):
    c = c_ref[...]
    a = c * jax.nn.sigmoid(c)
    a_hi, a_mid, a_lo = _split3(a)
    w = w_ref[...]
    w_hi, w_mid, w_lo = _split3(w)
    acc = _dot(a_hi, w_hi) + (_dot(a_hi, w_mid) + _dot(a_mid, w_hi))
    acc = acc + (_dot(a_mid, w_mid) + _dot(a_hi, w_lo) + _dot(a_lo, w_hi))
    o_ref[...] = acc + b_ref[...]


def _mod_call(c, w_mod, b_mod):
    bsz = c.shape[0]
    n = w_mod.shape[1]
    tn = 1024
    return pl.pallas_call(
        _mod_kernel,
        out_shape=jax.ShapeDtypeStruct((bsz, n), F32),
        grid=(n // tn,),
        in_specs=[
            pl.BlockSpec((bsz, D_MODEL), lambda j: (0, 0)),
            pl.BlockSpec((D_MODEL, tn), lambda j: (0, j)),
            pl.BlockSpec((1, tn), lambda j: (0, j)),
        ],
        out_specs=pl.BlockSpec((bsz, tn), lambda j: (0, j)),
        compiler_params=pltpu.CompilerParams(
            dimension_semantics=("parallel",), vmem_limit_bytes=VMEM_LIMIT),
        name="mod",
    )(c, w_mod, b_mod.reshape(1, n))


def _pre_kernel(x_ref, mod_ref, g1_ref, w_ref, gqa_ref, gka_ref, gqf_ref, gkf_ref, bf_ref,
                pos_ref, invf_ref, sign_ref, tri_ref, selq_ref, selk_ref, oneq_ref, onek_ref,
                qa_ref, ka_ref, va_ref, qf_ref, kf_ref, vf_ref, carry_ref):
    s_idx = pl.program_id(1)

    @pl.when(s_idx == 0)
    def _():
        carry_ref[...] = jnp.zeros_like(carry_ref)

    x = x_ref[0]
    mod = mod_ref[0]
    h = _rms_modulate(x, g1_ref[...], mod[1:2, :], mod[0:1, :]).astype(BF16)

    ang = pos_ref[0].astype(F32) * invf_ref[...]
    cos = jnp.cos(ang)
    sin_s = jnp.sin(ang) * sign_ref[...]
    lane = lax.broadcasted_iota(jnp.int32, (1, HEAD_PAD), 1)
    low_half = lane < HEAD_DIM // 2

    def head_norm(c, g):
        return c * lax.rsqrt(jnp.sum(c * c, axis=-1, keepdims=True) * (1.0 / HEAD_DIM) + EPS) * g

    def rope(c):
        partner = jnp.where(low_half, pltpu.roll(c, HEAD_PAD - HEAD_DIM // 2, 1),
                            pltpu.roll(c, HEAD_DIM // 2, 1))
        return c * cos + partner * sin_s

    scale = HEAD_DIM ** -0.5

    def proj_chunks(c0, width):
        for j in range(width // PROJ_W):
            wide = _dot(h, w_ref[:, c0 + j * PROJ_W:c0 + (j + 1) * PROJ_W])
            for i in range(PROJ_W // HEAD_PAD):
                yield wide[:, i * HEAD_PAD:(i + 1) * HEAD_PAD]

    for hh, c in enumerate(proj_chunks(C_QA, SWA_Q_HEADS * HEAD_PAD)):
        c = rope(head_norm(c, gqa_ref[:, hh * HEAD_PAD:(hh + 1) * HEAD_PAD])) * scale
        qa_ref[0, :, hh * HEAD_PAD:(hh + 1) * HEAD_PAD] = c.astype(BF16)
    for hh, c in enumerate(proj_chunks(C_KA, SWA_KV_HEADS * HEAD_PAD)):
        c = rope(head_norm(c, gka_ref[:, hh * HEAD_PAD:(hh + 1) * HEAD_PAD]))
        ka_ref[0, :, hh * HEAD_PAD:(hh + 1) * HEAD_PAD] = c.astype(BF16)
    va, f_logit = proj_chunks(C_VA, SWA_KV_W + HEAD_PAD)
    va_ref[0] = va.astype(BF16)
    for j, c in enumerate(proj_chunks(C_VF, FOX_W)):
        vf_ref[0, 0, j * HEAD_PAD:(j + 1) * HEAD_PAD, :] = c.T.astype(BF16)

    f_logit = f_logit + bf_ref[...]
    log_f = jnp.minimum(f_logit, 0.0) - jnp.log1p(jnp.exp(-jnp.abs(f_logit)))
    l_hi, l_mid, l_lo = _split3(log_f)
    tri = tri_ref[...]
    dcum = (_dot(tri, l_hi) + _dot(tri, l_mid) + _dot(tri, l_lo)) + carry_ref[...]
    tm = dcum.shape[0]
    carry_ref[...] = dcum[tm - 1:tm, :]

    d_hi, d_mid, d_lo = _split3(dcum)
    d3 = jnp.concatenate([d_hi, d_mid, d_lo], axis=1)
    aug_q = _dot(d3, selq_ref[...]) + oneq_ref[...]
    aug_k = onek_ref[...] - _dot(d3, selk_ref[...])

    for hh, c in enumerate(proj_chunks(C_QF, FOX_HEADS * HEAD_PAD)):
        sl = slice(hh * HEAD_PAD, (hh + 1) * HEAD_PAD)
        qf_ref[0, :, sl] = (head_norm(c, gqf_ref[:, sl]) * scale + aug_q[:, sl]).astype(BF16)
    for hh, c in enumerate(proj_chunks(C_KF, FOX_HEADS * HEAD_PAD)):
        sl = slice(hh * HEAD_PAD, (hh + 1) * HEAD_PAD)
        kf_ref[0, :, sl] = (head_norm(c, gkf_ref[:, sl]) + aug_k[:, sl]).astype(BF16)


def _pad_heads_cols(w, nheads):
    k = w.shape[0]
    w = w.reshape(k, nheads, HEAD_DIM)
    w = jnp.pad(w, ((0, 0), (0, 0), (0, HEAD_PAD - HEAD_DIM)))
    return w.reshape(k, nheads * HEAD_PAD)


def _pad_head_gain(g, nheads):
    g = jnp.pad(g.astype(F32), (0, HEAD_PAD - HEAD_DIM))
    return jnp.tile(g, nheads).reshape(1, nheads * HEAD_PAD)


def _decay_selectors():
    selq = np.zeros((3 * HEAD_PAD, FOX_HEADS * HEAD_PAD), np.float32)
    selk = np.zeros((3 * HEAD_PAD, FOX_HEADS * HEAD_PAD), np.float32)
    oneq = np.zeros((1, FOX_HEADS * HEAD_PAD), np.float32)
    onek = np.zeros((1, FOX_HEADS * HEAD_PAD), np.float32)
    for hh in range(FOX_HEADS):
        for p in range(3):
            selq[p * HEAD_PAD + hh, hh * HEAD_PAD + HEAD_DIM + p] = 1.0
            selk[p * HEAD_PAD + hh, hh * HEAD_PAD + HEAD_DIM + 3 + p] = 1.0
            oneq[0, hh * HEAD_PAD + HEAD_DIM + 3 + p] = 1.0
            onek[0, hh * HEAD_PAD + HEAD_DIM + p] = 1.0
    return (jnp.asarray(selq, BF16), jnp.asarray(selk, BF16), jnp.asarray(oneq), jnp.asarray(onek))


def _pre_call(x, mod3, g1, w1, gqa, gka, gqf, gkf, bf_pad, positions):
    bsz, seq, _ = x.shape
    tm = TM_PRE
    half = HEAD_DIM // 2
    inv_freq = ROPE_THETA ** (-jnp.arange(half, dtype=F32) / half)
    invf = jnp.concatenate([inv_freq, inv_freq, jnp.zeros((HEAD_PAD - HEAD_DIM,), F32)]).reshape(1, HEAD_PAD)
    sign = jnp.concatenate([-jnp.ones((half,), F32), jnp.ones((half,), F32),
                            jnp.zeros((HEAD_PAD - HEAD_DIM,), F32)]).reshape(1, HEAD_PAD)
    tri = jnp.asarray(np.tril(np.ones((tm, tm), np.float32)), BF16)
    selq, selk, oneq, onek = _decay_selectors()

    def full(a):
        return pl.BlockSpec(a.shape, lambda b, s: (0,) * a.ndim)

    def tok(width):
        return pl.BlockSpec((1, tm, width), lambda b, s: (b, s, 0))

    consts = (g1, w1, gqa, gka, gqf, gkf, bf_pad)
    tail = (invf, sign, tri, selq, selk, oneq, onek)
    out_w = (SWA_Q_HEADS * HEAD_PAD, SWA_KV_HEADS * HEAD_PAD, SWA_KV_W,
             FOX_HEADS * HEAD_PAD, FOX_HEADS * HEAD_PAD)
    assert tm == TQ_FOX
    return pl.pallas_call(
        _pre_kernel,
        out_shape=[jax.ShapeDtypeStruct((bsz, seq, w), BF16) for w in out_w]
        + [jax.ShapeDtypeStruct((bsz, seq // tm, FOX_W, tm), BF16)],
        grid=(bsz, seq // tm),
        in_specs=[tok(D_MODEL), pl.BlockSpec((1, N_MOD, D_MODEL), lambda b, s: (b, 0, 0))]
        + [full(a) for a in consts] + [tok(1)] + [full(a) for a in tail],
        out_specs=[tok(w) for w in out_w]
        + [pl.BlockSpec((1, 1, FOX_W, tm), lambda b, s: (b, s, 0, 0))],
        scratch_shapes=[pltpu.VMEM((1, HEAD_PAD), F32)],
        compiler_params=pltpu.CompilerParams(
            dimension_semantics=("parallel", "arbitrary"), vmem_limit_bytes=VMEM_LIMIT),
        name="pre",
    )(x, mod3, *consts, positions.reshape(bsz, seq, 1), *tail)


def _swa_kernel(sink_ref, q_ref, kc_ref, kp_ref, vc_ref, vp_ref, o_ref):
    t_idx = pl.program_id(1)
    tq = q_ref.shape[1]
    nsub = tq // BLOCK
    k_all = jnp.concatenate([kp_ref[0], kc_ref[0]], axis=0)
    v_all = jnp.concatenate([vp_ref[0], vc_ref[0]], axis=0)
    v_swap = jnp.concatenate([v_all[:, HEAD_DIM:], v_all[:, :HEAD_DIM]], axis=1)

    rows = SWA_GROUP * BLOCK
    r = lax.broadcasted_iota(jnp.int32, (rows, 2 * BLOCK), 0)
    kcol = lax.broadcasted_iota(jnp.int32, (rows, 2 * BLOCK), 1)
    rel = (r % BLOCK) + BLOCK - kcol
    in_window = (rel >= 0) & (rel < SWA_WINDOW)
    rgrp = lax.broadcasted_iota(jnp.int32, (rows, 1), 0) // BLOCK
    lane_lo = lax.broadcasted_iota(jnp.int32, (1, HEAD_PAD), 1) < HEAD_DIM

    for j in range(nsub):
        lo_key = jnp.where(t_idx == 0, BLOCK, 0) if j == 0 else 0
        mask = in_window & (kcol >= lo_key)
        vb = v_all[j * BLOCK:(j + 2) * BLOCK, :]
        vsb = v_swap[j * BLOCK:(j + 2) * BLOCK, :]
        for g in range(SWA_KV_HEADS):
            kb = k_all[j * BLOCK:(j + 2) * BLOCK, g * HEAD_PAD:(g + 1) * HEAD_PAD]
            qg = jnp.concatenate(
                [q_ref[0, j * BLOCK:(j + 1) * BLOCK,
                       (g * SWA_GROUP + u) * HEAD_PAD:(g * SWA_GROUP + u + 1) * HEAD_PAD]
                 for u in range(SWA_GROUP)], axis=0)
            s = jnp.where(mask, _dot_nt(qg, kb), NEG_INF)
            sink = jnp.zeros((rows, 1), F32)
            for u in range(SWA_GROUP):
                sink = jnp.where(rgrp == u, sink_ref[g * SWA_GROUP + u], sink)
            m = jnp.maximum(jnp.max(s, axis=-1, keepdims=True), sink)
            p = jnp.exp(s - m)
            denom = jnp.sum(p, axis=-1, keepdims=True) + jnp.exp(sink - m)
            pb = p.astype(BF16)
            inv = 1.0 / denom
            o_nat = _dot(pb, vb) * inv
            o_swp = _dot(pb, vsb) * inv
            o_lo, o_hi = (o_nat, o_swp) if g == 0 else (o_swp, o_nat)
            for w in range(SWA_GROUP // 2):
                ev = o_lo[(2 * w) * BLOCK:(2 * w + 1) * BLOCK, :]
                od = o_hi[(2 * w + 1) * BLOCK:(2 * w + 2) * BLOCK, :]
                pc = g * (SWA_GROUP // 2) + w
                o_ref[0, j * BLOCK:(j + 1) * BLOCK, pc * HEAD_PAD:(pc + 1) * HEAD_PAD] = (
                    jnp.where(lane_lo, ev, od).astype(BF16))


def _swa_call(qa, ka, va, sinks):
    bsz, seq, _ = qa.shape
    tq = TQ_SWA
    per = tq // BLOCK

    def cur(width):
        return pl.BlockSpec((1, tq, width), lambda b, t: (b, t, 0))

    def prev(width):
        return pl.BlockSpec((1, BLOCK, width), lambda b, t: (b, jnp.maximum(t * per - 1, 0), 0))

    return pl.pallas_call(
        _swa_kernel,
        out_shape=jax.ShapeDtypeStruct((bsz, seq, SWA_Q_W), BF16),
        grid=(bsz, seq // tq),
        in_specs=[pl.BlockSpec(memory_space=pltpu.SMEM),
                  cur(SWA_Q_HEADS * HEAD_PAD),
                  cur(SWA_KV_HEADS * HEAD_PAD), prev(SWA_KV_HEADS * HEAD_PAD),
                  cur(SWA_KV_W), prev(SWA_KV_W)],
        out_specs=cur(SWA_Q_W),
        compiler_params=pltpu.CompilerParams(
            dimension_semantics=("parallel", "parallel"), vmem_limit_bytes=VMEM_LIMIT),
        name="swa",
    )(sinks.astype(F32), qa, ka, ka, va, va)


def _fox_kernel(q_ref, k_ref, vt_ref, o_ref):
    qi = pl.program_id(2)
    tq = q_ref.shape[1]
    tk = tq
    nq = tq // FOX_CHAIN_Q
    key = lax.broadcasted_iota(jnp.int32, (tk, FOX_CHAIN_Q), 0)
    qry = lax.broadcasted_iota(jnp.int32, (tk, FOX_CHAIN_Q), 1)

    def step(j, carry, masked):
        start = pl.multiple_of(j * tk, tk)
        chains = [(e, c) for e in range(2) for c in range(nq)]

        def scores(e, c):
            k = k_ref[0, pl.ds(start, tk), e * HEAD_PAD:(e + 1) * HEAD_PAD]
            q = q_ref[0, c * FOX_CHAIN_Q:(c + 1) * FOX_CHAIN_Q, e * HEAD_PAD:(e + 1) * HEAD_PAD]
            s = _dot_nt(k, q)
            if masked:
                s = jnp.where(key <= qry + c * FOX_CHAIN_Q, s, NEG_INF)
            return s

        out = []
        s_next = scores(*chains[0])
        for idx, (e, c) in enumerate(chains):
            s = s_next
            if idx + 1 < len(chains):
                s_next = scores(*chains[idx + 1])
            m, l, acc = carry[idx]
            m_new = jnp.maximum(m, jnp.max(s, axis=0, keepdims=True))
            alpha = jnp.exp(m - m_new)
            p = jnp.exp(s - m_new)
            l = alpha * l + jnp.sum(p, axis=0, keepdims=True)
            vt = vt_ref[0, j, e * HEAD_DIM:(e + 1) * HEAD_DIM, :]
            acc = alpha * acc + _dot(vt, p.astype(BF16))
            out.append((m_new, l, acc))
        return tuple(out)

    init1 = (jnp.full((1, FOX_CHAIN_Q), NEG_INF, F32), jnp.zeros((1, FOX_CHAIN_Q), F32),
             jnp.zeros((HEAD_DIM, FOX_CHAIN_Q), F32))
    carry = lax.fori_loop(0, qi, functools.partial(step, masked=False), (init1,) * (2 * nq))
    fin = step(qi, carry, True)
    o_t = jnp.concatenate(
        [jnp.concatenate([fin[e * nq + c][2] * (1.0 / fin[e * nq + c][1]) for c in range(nq)], axis=1)
         for e in range(2)], axis=0)
    o_ref[0] = o_t.T.astype(BF16)


def _fox_call(qf, kf, vft):
    bsz, seq, _ = qf.shape
    tq = TQ_FOX
    pairs = FOX_HEADS // 2
    return pl.pallas_call(
        _fox_kernel,
        out_shape=jax.ShapeDtypeStruct((bsz, seq, FOX_W), BF16),
        grid=(bsz, pairs, seq // tq),
        in_specs=[pl.BlockSpec((1, tq, 2 * HEAD_PAD), lambda b, p, t: (b, t, p)),
                  pl.BlockSpec((1, seq, 2 * HEAD_PAD), lambda b, p, t: (b, 0, p)),
                  pl.BlockSpec((1, seq // tq, 2 * HEAD_DIM, tq), lambda b, p, t: (b, 0, p, 0))],
        out_specs=pl.BlockSpec((1, tq, 2 * HEAD_DIM), lambda b, p, t: (b, t, p)),
        compiler_params=pltpu.CompilerParams(
            dimension_semantics=("parallel", "parallel", "arbitrary"),
            vmem_limit_bytes=VMEM_LIMIT),
        name="fox",
    )(qf, kf, vft)


def _post_kernel(x_ref, mod_ref, g1_ref, wg_ref, oa_ref, ob_ref, wa_ref, wb_ref, wo_ref, g2_ref,
                 x1_ref, h2_ref):
    x = x_ref[0]
    mod = mod_ref[0]
    h = _rms_modulate(x, g1_ref[...], mod[1:2, :], mod[0:1, :]).astype(BF16)
    gate_a = jax.nn.sigmoid(_dot(h, wg_ref[:, :D_MODEL]))
    gate_b = jax.nn.sigmoid(_dot(h, wg_ref[:, D_MODEL:]))
    merged = gate_a * _dot(oa_ref[0], wa_ref[...]) + gate_b * _dot(ob_ref[0], wb_ref[...])
    x1 = x + mod[2:3, :] * _dot(merged.astype(BF16), wo_ref[...])
    x1_ref[0] = x1
    h2_ref[0] = _rms_modulate(x1, g2_ref[...], mod[4:5, :], mod[3:4, :]).astype(BF16)


def _post_call(x, mod3, g1, wg, out_a, out_b, wa, wb, wo, g2):
    bsz, seq, _ = x.shape
    tm = TM_PRE

    def full(a):
        return pl.BlockSpec(a.shape, lambda b, s: (0,) * a.ndim)

    def tok(width):
        return pl.BlockSpec((1, tm, width), lambda b, s: (b, s, 0))

    return pl.pallas_call(
        _post_kernel,
        out_shape=[jax.ShapeDtypeStruct((bsz, seq, D_MODEL), F32),
                   jax.ShapeDtypeStruct((bsz, seq, D_MODEL), BF16)],
        grid=(bsz, seq // tm),
        in_specs=[tok(D_MODEL), pl.BlockSpec((1, N_MOD, D_MODEL), lambda b, s: (b, 0, 0)),
                  full(g1), full(wg), tok(SWA_Q_W), tok(FOX_W), full(wa), full(wb), full(wo),
                  full(g2)],
        out_specs=[tok(D_MODEL), tok(D_MODEL)],
        compiler_params=pltpu.CompilerParams(
            dimension_semantics=("parallel", "parallel"), vmem_limit_bytes=VMEM_LIMIT),
        name="post",
    )(x, mod3, g1, wg, out_a, out_b, wa, wb, wo, g2)


def _top16(s):
    cur = s
    rank = jnp.full(s.shape, float(PEER_TOPK), F32)
    vals = []
    for r in range(PEER_TOPK):
        mx = jnp.max(cur, axis=0, keepdims=True)
        vals.append(mx)
        hit = cur == mx
        rank = jnp.where(hit, float(r), rank)
        cur = jnp.where(hit, -3.0e38, cur)
    return vals, rank


CAND_LOW_B = (16, 8, 5, 4)
CAND_HIGH_A = (15, 7, 4)


def _route_chunk(s0, s1):
    v0, rank0 = _top16(s0)
    v1, rank1 = _top16(s1)
    v0_all = jnp.concatenate(v0, axis=0)
    v1_all = jnp.concatenate(v1, axis=0)
    idx = lax.broadcasted_iota(jnp.int32, (PEER_TOPK, 1), 0)
    low = [jnp.where(idx < CAND_LOW_B[a], v0[a] + v1_all, -3.0e38) for a in range(len(CAND_LOW_B))]
    high = [jnp.where((idx >= len(CAND_LOW_B)) & (idx <= CAND_HIGH_A[b]), v0_all + v1[b], -3.0e38)
            for b in range(len(CAND_HIGH_A))]
    cand = jnp.concatenate(low + high, axis=0)
    cur = cand
    tau = None
    for r in range(PEER_TOPK):
        tau = jnp.max(cur, axis=0, keepdims=True)
        cur = jnp.where(cur == tau, -3.0e38, cur)
    top = v0[0] + v1[0]
    sel = jnp.where(cand >= tau, 1.0, 0.0)
    z = jnp.sum(sel * jnp.exp(cand - top), axis=0, keepdims=True)
    n_low = len(CAND_LOW_B)
    n_high = sel[n_low * PEER_TOPK:(n_low + 1) * PEER_TOPK, :]
    for b in range(1, len(CAND_HIGH_A)):
        n_high = n_high + sel[(n_low + b) * PEER_TOPK:(n_low + b + 1) * PEER_TOPK, :]
    cnt = jnp.zeros_like(s0)
    for a in range(PEER_TOPK):
        if a < n_low:
            n_a = jnp.sum(sel[a * PEER_TOPK:(a + 1) * PEER_TOPK, :], axis=0, keepdims=True)
        else:
            n_a = n_high[a:a + 1, :]
        cnt = jnp.where(rank0 == float(a), n_a, cnt)
    return cnt, jnp.exp(s0 - v0[0]) * (0.5 / z), rank1, jnp.exp(s1 - v1[0])


def _route_kernel(h_ref, wq_ref, keys_ref, n_ref, c0_ref, r1_ref, e1_ref, qt_ref, s_ref):
    qt_ref[...] = _dot_nt(wq_ref[...], h_ref[...]).astype(BF16)
    tt = h_ref.shape[0]

    def head(hh, _):
        base = pl.multiple_of(hh * PEER_QDIM, PEER_QDIM)
        s_ref[0] = _dot(keys_ref[2 * hh], qt_ref[pl.ds(base, PEER_HALF), :])
        s_ref[1] = _dot(keys_ref[2 * hh + 1], qt_ref[pl.ds(base + PEER_HALF, PEER_HALF), :])
        rows = pl.ds(pl.multiple_of(hh * PEER_NKEYS, PEER_NKEYS), PEER_NKEYS)
        half = pl.ds(pl.multiple_of(hh * (PEER_NKEYS // 2), PEER_NKEYS // 2), PEER_NKEYS // 2)
        parts = [_route_chunk(s_ref[0, :, lc * 128:(lc + 1) * 128],
                              s_ref[1, :, lc * 128:(lc + 1) * 128]) for lc in range(tt // 128)]
        cnt, c0, rank1, e1 = (jnp.concatenate(t, axis=1) for t in zip(*parts))
        n_ref[rows, :] = cnt
        c0_ref[rows, :] = c0
        r1_ref[half, :] = pltpu.bitcast(rank1.astype(BF16), jnp.uint32)
        e1_ref[half, :] = pltpu.bitcast(e1.astype(BF16), jnp.uint32)
        return 0

    lax.fori_loop(0, PEER_HEADS, head, 0)


def _route_call(h2, wq_t, keys):
    t_all = h2.shape[0]
    tt = TT_ROUTE
    rows = PEER_HEADS * PEER_NKEYS
    out = jax.ShapeDtypeStruct((rows, t_all), F32)
    out_b = jax.ShapeDtypeStruct((rows // 2, t_all), jnp.uint32)
    return pl.pallas_call(
        _route_kernel,
        out_shape=[out, out, out_b, out_b],
        grid=(t_all // tt,),
        in_specs=[pl.BlockSpec((tt, D_MODEL), lambda t: (t, 0)),
                  pl.BlockSpec(wq_t.shape, lambda t: (0, 0)),
                  pl.BlockSpec(keys.shape, lambda t: (0, 0, 0))],
        out_specs=[pl.BlockSpec((rows, tt), lambda t: (0, t))] * 2
        + [pl.BlockSpec((rows // 2, tt), lambda t: (0, t))] * 2,
        scratch_shapes=[pltpu.VMEM((PEER_HEADS * PEER_QDIM, tt), BF16),
                        pltpu.VMEM((2, PEER_NKEYS, tt), F32)],
        compiler_params=pltpu.CompilerParams(
            dimension_semantics=("parallel",), vmem_limit_bytes=VMEM_LIMIT),
        name="route",
    )(h2, wq_t, keys)


def _peer_tail_piece(k, acc_ref, vt_half_ref, w_half_ref):
    rows = slice(k * (D_MODEL // 4), (k + 1) * (D_MODEL // 4))
    acc_ref[rows, :] += _dot(vt_half_ref[0, rows, :], w_half_ref[...])


def _peer_main(e_idx, h_ref, u_ref, vta_ref, vtb_prev_ref, n_ref, c0_ref, r1_ref, e1_ref,
               acc_ref, nsel_ref, csel_ref, a_ref, w0_ref, w1_ref, ht_ref):
    te, tt = a_ref.shape
    blocks = te // PEER_NKEYS
    assert blocks == 8
    tile3 = (PEER_NKEYS // BF16_ROWS, BF16_ROWS, 128)
    w_refs = (w0_ref, w1_ref)

    for hh in range(PEER_HEADS):
        grp = pl.ds(pl.multiple_of(hh * PEER_NKEYS + e_idx * blocks, 8), 8)
        nsel_ref[hh * 8:(hh + 1) * 8, :] = n_ref[grp, :]
        csel_ref[hh * 8:(hh + 1) * 8, :] = c0_ref[grp, :]

    def scores(ii):
        rows = slice(ii * PEER_NKEYS, (ii + 1) * PEER_NKEYS)
        a_ref[rows, :] = _dot(u_ref[rows, :], ht_ref[...])

    for ii in range(PEER_AHEAD):
        scores(ii)
    for ii in range(blocks):
        if ii + PEER_AHEAD < blocks:
            scores(ii + PEER_AHEAD)
        if ii < blocks // 2:
            _peer_tail_piece(ii, acc_ref, vtb_prev_ref, w1_ref)
        else:
            _peer_tail_piece(ii - blocks // 2, acc_ref, vta_ref, w0_ref)
        rows = slice(ii * PEER_NKEYS, (ii + 1) * PEER_NKEYS)
        for lc in range(tt // 128):
            lanes = slice(lc * 128, (lc + 1) * 128)
            p = None
            for hh in range(PEER_HEADS):
                n_row = nsel_ref[hh * 8 + ii:hh * 8 + ii + 1, lanes]
                c_row = csel_ref[hh * 8 + ii:hh * 8 + ii + 1, lanes]
                n_b = jnp.broadcast_to(n_row, (BF16_ROWS, 128)).astype(BF16)[None]
                c_b = jnp.broadcast_to(c_row, (BF16_ROWS, 128)).astype(BF16)[None]
                words = slice(hh * (PEER_NKEYS // 2), (hh + 1) * (PEER_NKEYS // 2))
                r1 = pltpu.bitcast(r1_ref[words, lanes], BF16).reshape(tile3)
                e1 = pltpu.bitcast(e1_ref[words, lanes], BF16).reshape(tile3)
                term = jnp.where(r1 < n_b, e1 * c_b, jnp.zeros((), BF16))
                p = term if p is None else p + term
            a = a_ref[rows, lanes]
            gelu2 = a * (1.0 + lax.erf(a * (2.0 ** -0.5)))
            w_half = w_refs[ii // (blocks // 2)]
            hrows = slice((ii % (blocks // 2)) * PEER_NKEYS, (ii % (blocks // 2) + 1) * PEER_NKEYS)
            w_half[hrows, lanes] = p.reshape(PEER_NKEYS, 128) * gelu2.astype(BF16)


def _peer_kernel(h_ref, u_ref, vta_ref, vtb_prev_ref, n_ref, c0_ref, r1_ref, e1_ref, x1_ref,
                 mod_ref, o_ref, acc_ref, nsel_ref, csel_ref, a_ref, w0_ref, w1_ref, ht_ref):
    e_idx = pl.program_id(1)
    n_tiles = pl.num_programs(1) - 1

    @pl.when(e_idx == 0)
    def _():
        acc_ref[...] = jnp.zeros_like(acc_ref)
        w1_ref[...] = jnp.zeros_like(w1_ref)
        ht_ref[...] = h_ref[...].astype(F32).T.astype(BF16)

    @pl.when(e_idx < n_tiles)
    def _():
        _peer_main(e_idx, h_ref, u_ref, vta_ref, vtb_prev_ref, n_ref, c0_ref, r1_ref, e1_ref,
                   acc_ref, nsel_ref, csel_ref, a_ref, w0_ref, w1_ref, ht_ref)

    @pl.when(e_idx == n_tiles)
    def _():
        for k in range(4):
            _peer_tail_piece(k, acc_ref, vtb_prev_ref, w1_ref)
        o_ref[...] = x1_ref[...] + mod_ref[0][5:6, :] * acc_ref[...].T


def _peer_call(h2, u_b, vt_b, n_t, c0_t, r1_t, e1_t, x1, mod3, tiles_per_batch):
    t_all = h2.shape[0]
    tt, te = TT_PEER, TE_PEER
    rows = PEER_HEADS * PEER_NKEYS

    n_tiles = PEER_EXPERTS // te
    last = n_tiles - 1

    def tab(nrows=rows):
        return pl.BlockSpec((nrows, tt), lambda t, e: (0, t))

    return pl.pallas_call(
        _peer_kernel,
        out_shape=jax.ShapeDtypeStruct((t_all, D_MODEL), F32),
        grid=(t_all // tt, n_tiles + 1),
        in_specs=[pl.BlockSpec((tt, D_MODEL), lambda t, e: (t, 0)),
                  pl.BlockSpec((te, D_MODEL), lambda t, e: (jnp.minimum(e, last), 0)),
                  pl.BlockSpec((1, D_MODEL, te // 2), lambda t, e: (jnp.minimum(e, last), 0, 0)),
                  pl.BlockSpec((1, D_MODEL, te // 2), lambda t, e: (jnp.maximum(e - 1, 0), 0, 1)),
                  tab(), tab(), tab(rows // 2), tab(rows // 2),
                  pl.BlockSpec((tt, D_MODEL), lambda t, e: (t, 0)),
                  pl.BlockSpec((1, N_MOD, D_MODEL), lambda t, e: (t // tiles_per_batch, 0, 0))],
        out_specs=pl.BlockSpec((tt, D_MODEL), lambda t, e: (t, 0)),
        scratch_shapes=[pltpu.VMEM((D_MODEL, tt), F32),
                        pltpu.VMEM((PEER_HEADS * 8, tt), F32), pltpu.VMEM((PEER_HEADS * 8, tt), F32),
                        pltpu.VMEM((te, tt), F32),
                        pltpu.VMEM((te // 2, tt), BF16), pltpu.VMEM((te // 2, tt), BF16),
                        pltpu.VMEM((D_MODEL, tt), BF16)],
        compiler_params=pltpu.CompilerParams(
            dimension_semantics=("parallel", "arbitrary"), vmem_limit_bytes=VMEM_LIMIT),
        name="peer",
    )(h2, u_b, vt_b, vt_b, n_t, c0_t, r1_t, e1_t, x1, mod3)


def _layer(x, c, positions, w_mod, b_mod, norm1_g, w_in, q_norm_swa, k_norm_swa, sinks,
           q_norm_fox, k_norm_fox, b_forget, w_out_swa, w_out_fox, w_o, norm2_g,
           peer_w_query, peer_sub_keys, peer_u, peer_v):
    bsz, seq, _ = x.shape
    split_at = [int(v) for v in np.cumsum(IN_SPLITS)[:-1]]
    w_qa, w_ka, w_va, w_qf, w_kf, w_vf, w_f, w_ga, w_gb = jnp.split(w_in, split_at, axis=1)
    w1 = jnp.concatenate(
        [_pad_heads_cols(w_qa, SWA_Q_HEADS), _pad_heads_cols(w_ka, SWA_KV_HEADS), w_va,
         jnp.pad(w_f, ((0, 0), (0, HEAD_PAD - FOX_HEADS))),
         _pad_heads_cols(w_qf, FOX_HEADS), _pad_heads_cols(w_kf, FOX_HEADS), w_vf],
        axis=1).astype(BF16)
    wg = jnp.concatenate([w_ga, w_gb], axis=1).astype(BF16)
    bf_pad = jnp.pad(b_forget.astype(F32), (0, HEAD_PAD - FOX_HEADS)).reshape(1, HEAD_PAD)
    g1 = norm1_g.astype(F32).reshape(1, D_MODEL)
    g2 = norm2_g.astype(F32).reshape(1, D_MODEL)

    mod3 = _mod_call(c, w_mod, b_mod).reshape(bsz, N_MOD, D_MODEL)

    qa, ka, va, qf, kf, vf = _pre_call(
        x, mod3, g1, w1,
        _pad_head_gain(q_norm_swa, SWA_Q_HEADS), _pad_head_gain(k_norm_swa, SWA_KV_HEADS),
        _pad_head_gain(q_norm_fox, FOX_HEADS), _pad_head_gain(k_norm_fox, FOX_HEADS),
        bf_pad, positions)

    out_a = _swa_call(qa, ka, va, sinks)
    out_b = _fox_call(qf, kf, vf)

    x1, h2 = _post_call(x, mod3, g1, wg, out_a, out_b, w_out_swa.astype(BF16),
                        w_out_fox.astype(BF16), w_o.astype(BF16), g2)

    t_all = bsz * seq
    h2f = h2.reshape(t_all, D_MODEL)
    keys = peer_sub_keys.reshape(PEER_HEADS * 2, PEER_NKEYS, PEER_HALF).astype(BF16)
    n_t, c0_t, r1_t, e1_t = _route_call(h2f, peer_w_query.T.astype(BF16), keys)
    vt_tiles = peer_v.astype(BF16).reshape(PEER_EXPERTS // TE_PEER, TE_PEER, D_MODEL)
    vt_tiles = vt_tiles.transpose(0, 2, 1)
    out = _peer_call(h2f, peer_u.astype(BF16), vt_tiles, n_t, c0_t, r1_t, e1_t,
                     x1.reshape(t_all, D_MODEL), mod3, seq // TT_PEER)
    return out.reshape(bsz, seq, D_MODEL)


def kernel(x, c, positions, w_mod, b_mod, norm1_g, w_in, q_norm_swa, k_norm_swa, sinks, q_norm_fox, k_norm_fox, b_forget, w_out_swa, w_out_fox, w_o, norm2_g, peer_w_query, peer_sub_keys, peer_u, peer_v):
    for l in range(w_mod.shape[0]):
        x = _layer(x, c, positions, w_mod[l], b_mod[l], norm1_g[l], w_in[l], q_norm_swa[l],
                   k_norm_swa[l], sinks[l], q_norm_fox[l], k_norm_fox[l], b_forget[l],
                   w_out_swa[l], w_out_fox[l], w_o[l], norm2_g[l], peer_w_query[l],
                   peer_sub_keys[l], peer_u[l], peer_v[l])
    return x
```

```python
import functools

import jax
import jax.numpy as jnp
import numpy as np
from jax import lax
from jax.experimental import pallas as pl
from jax.experimental.pallas import tpu as pltpu

D_MODEL = 1024
HEAD_DIM = 64
HEAD_PAD = 128
SWA_Q_HEADS = 8
SWA_KV_HEADS = 2
SWA_GROUP = SWA_Q_HEADS // SWA_KV_HEADS
SWA_WINDOW = 128
FOX_HEADS = 8
BLOCK = 128
ROPE_THETA = 10000.0
PEER_HEADS = 8
PEER_NKEYS = 128
PEER_EXPERTS = PEER_NKEYS * PEER_NKEYS
PEER_QDIM = 256
PEER_HALF = PEER_QDIM // 2
PEER_TOPK = 16
N_MOD = 6
EPS = 1e-6
NEG_INF = -1e30

SWA_Q_W = SWA_Q_HEADS * HEAD_DIM
SWA_KV_W = SWA_KV_HEADS * HEAD_DIM
FOX_W = FOX_HEADS * HEAD_DIM
IN_SPLITS = (SWA_Q_W, SWA_KV_W, SWA_KV_W, FOX_W, FOX_W, FOX_W, FOX_HEADS, D_MODEL, D_MODEL)

C_QA = 0
C_KA = C_QA + SWA_Q_HEADS * HEAD_PAD
C_VA = C_KA + SWA_KV_HEADS * HEAD_PAD
C_F = C_VA + SWA_KV_W
C_QF = C_F + HEAD_PAD
C_KF = C_QF + FOX_HEADS * HEAD_PAD
C_VF = C_KF + FOX_HEADS * HEAD_PAD
C_END = C_VF + FOX_W
PROJ_W = 256

TM_PRE = 512
TQ_SWA = 512
TQ_FOX = 512
FOX_CHAIN_Q = 256
FOX_AHEAD = 2
TT_ROUTE = 256
TT_PEER = 512
TE_PEER = 1024
VMEM_LIMIT = 56 * 1024 * 1024

BF16 = jnp.bfloat16
F32 = jnp.float32
BF16_ROWS = 16


def _split3(v):
    hi = v.astype(BF16)
    r1 = v - hi.astype(F32)
    mid = r1.astype(BF16)
    lo = (r1 - mid.astype(F32)).astype(BF16)
    return hi, mid, lo


def _dot_nt(a, b):
    return lax.dot_general(a, b, (((1,), (1,)), ((), ())), preferred_element_type=F32)


def _dot(a, b):
    return jnp.dot(a, b, preferred_element_type=F32)


def _rms_modulate(x, g, scale, shift):
    y = x * lax.rsqrt(jnp.mean(x * x, axis=-1, keepdims=True) + EPS)
    return (y * g) * (1.0 + scale) + shift


def _mod_kernel(c_ref, w_ref, b_ref, o_ref):
    c = c_ref[...]
    a = c * jax.nn.sigmoid(c)
    a_hi, a_mid, a_lo = _split3(a)
    w = w_ref[...]
    w_hi, w_mid, w_lo = _split3(w)
    acc = _dot(a_hi, w_hi) + (_dot(a_hi, w_mid) + _dot(a_mid, w_hi))
    acc = acc + (_dot(a_mid, w_mid) + _dot(a_hi, w_lo) + _dot(a_lo, w_hi))
    o_ref[...] = acc + b_ref[...]


def _mod_call(c, w_mod, b_mod):
    bsz = c.shape[0]
    n = w_mod.shape[1]
    tn = 1024
    return pl.pallas_call(
        _mod_kernel,
        out_shape=jax.ShapeDtypeStruct((bsz, n), F32),
        grid=(n // tn,),
        in_specs=[
            pl.BlockSpec((bsz, D_MODEL), lambda j: (0, 0)),
            pl.BlockSpec((D_MODEL, tn), lambda j: (0, j)),
            pl.BlockSpec((1, tn), lambda j: (0, j)),
        ],
        out_specs=pl.BlockSpec((bsz, tn), lambda j: (0, j)),
        compiler_params=pltpu.CompilerParams(
            dimension_semantics=("parallel",), vmem_limit_bytes=VMEM_LIMIT),
        name="mod",
    )(c, w_mod, b_mod.reshape(1, n))


def _pre_kernel(x_ref, mod_ref, g1_ref, w_ref, gqa_ref, gka_ref, gqf_ref, gkf_ref, bf_ref,
                pos_ref, invf_ref, sign_ref, tri_ref, selq_ref, selk_ref, oneq_ref, onek_ref,
                qa_ref, ka_ref, va_ref, qf_ref, kf_ref, vf_ref, carry_ref):
    s_idx = pl.program_id(1)

    @pl.when(s_idx == 0)
    def _():
        carry_ref[...] = jnp.zeros_like(carry_ref)

    x = x_ref[0]
    mod = mod_ref[0]
    h = _rms_modulate(x, g1_ref[...], mod[1:2, :], mod[0:1, :]).astype(BF16)

    ang = pos_ref[0].astype(F32) * invf_ref[...]
    cos = jnp.cos(ang)
    sin_s = jnp.sin(ang) * sign_ref[...]
    lane = lax.broadcasted_iota(jnp.int32, (1, HEAD_PAD), 1)
    low_half = lane < HEAD_DIM // 2

    def head_norm(c, g):
        return c * lax.rsqrt(jnp.sum(c * c, axis=-1, keepdims=True) * (1.0 / HEAD_DIM) + EPS) * g

    def rope(c):
        partner = jnp.where(low_half, pltpu.roll(c, HEAD_PAD - HEAD_DIM // 2, 1),
                            pltpu.roll(c, HEAD_DIM // 2, 1))
        return c * cos + partner * sin_s

    scale = HEAD_DIM ** -0.5

    def proj_chunks(c0, width):
        for j in range(width // PROJ_W):
            wide = _dot(h, w_ref[:, c0 + j * PROJ_W:c0 + (j + 1) * PROJ_W])
            for i in range(PROJ_W // HEAD_PAD):
                yield wide[:, i * HEAD_PAD:(i + 1) * HEAD_PAD]

    for hh, c in enumerate(proj_chunks(C_QA, SWA_Q_HEADS * HEAD_PAD)):
        c = rope(head_norm(c, gqa_ref[:, hh * HEAD_PAD:(hh + 1) * HEAD_PAD])) * scale
        qa_ref[0, :, hh * HEAD_PAD:(hh + 1) * HEAD_PAD] = c.astype(BF16)
    for hh, c in enumerate(proj_chunks(C_KA, SWA_KV_HEADS * HEAD_PAD)):
        c = rope(head_norm(c, gka_ref[:, hh * HEAD_PAD:(hh + 1) * HEAD_PAD]))
        ka_ref[0, :, hh * HEAD_PAD:(hh + 1) * HEAD_PAD] = c.astype(BF16)
    va, f_logit = proj_chunks(C_VA, SWA_KV_W + HEAD_PAD)
    va_ref[0] = va.astype(BF16)
    for j, c in enumerate(proj_chunks(C_VF, FOX_W)):
        vf_ref[0, 0, j * HEAD_PAD:(j + 1) * HEAD_PAD, :] = c.T.astype(BF16)

    f_logit = f_logit + bf_ref[...]
    log_f = jnp.minimum(f_logit, 0.0) - jnp.log1p(jnp.exp(-jnp.abs(f_logit)))
    l_hi, l_mid, l_lo = _split3(log_f)
    tri = tri_ref[...]
    dcum = (_dot(tri, l_hi) + _dot(tri, l_mid) + _dot(tri, l_lo)) + carry_ref[...]
    tm = dcum.shape[0]
    carry_ref[...] = dcum[tm - 1:tm, :]

    d_hi, d_mid, d_lo = _split3(dcum)
    d3 = jnp.concatenate([d_hi, d_mid, d_lo], axis=1)
    aug_q = _dot(d3, selq_ref[...]) + oneq_ref[...]
    aug_k = onek_ref[...] - _dot(d3, selk_ref[...])

    for hh, c in enumerate(proj_chunks(C_QF, FOX_HEADS * HEAD_PAD)):
        sl = slice(hh * HEAD_PAD, (hh + 1) * HEAD_PAD)
        qf_ref[0, :, sl] = (head_norm(c, gqf_ref[:, sl]) * scale + aug_q[:, sl]).astype(BF16)
    for hh, c in enumerate(proj_chunks(C_KF, FOX_HEADS * HEAD_PAD)):
        sl = slice(hh * HEAD_PAD, (hh + 1) * HEAD_PAD)
        kf_ref[0, :, sl] = (head_norm(c, gkf_ref[:, sl]) + aug_k[:, sl]).astype(BF16)


def _pad_heads_cols(w, nheads):
    k = w.shape[0]
    w = w.reshape(k, nheads, HEAD_DIM)
    w = jnp.pad(w, ((0, 0), (0, 0), (0, HEAD_PAD - HEAD_DIM)))
    return w.reshape(k, nheads * HEAD_PAD)


def _pad_head_gain(g, nheads):
    g = jnp.pad(g.astype(F32), (0, HEAD_PAD - HEAD_DIM))
    return jnp.tile(g, nheads).reshape(1, nheads * HEAD_PAD)


def _decay_selectors():
    selq = np.zeros((3 * HEAD_PAD, FOX_HEADS * HEAD_PAD), np.float32)
    selk = np.zeros((3 * HEAD_PAD, FOX_HEADS * HEAD_PAD), np.float32)
    oneq = np.zeros((1, FOX_HEADS * HEAD_PAD), np.float32)
    onek = np.zeros((1, FOX_HEADS * HEAD_PAD), np.float32)
    for hh in range(FOX_HEADS):
        for p in range(3):
            selq[p * HEAD_PAD + hh, hh * HEAD_PAD + HEAD_DIM + p] = 1.0
            selk[p * HEAD_PAD + hh, hh * HEAD_PAD + HEAD_DIM + 3 + p] = 1.0
            oneq[0, hh * HEAD_PAD + HEAD_DIM + 3 + p] = 1.0
            onek[0, hh * HEAD_PAD + HEAD_DIM + p] = 1.0
    return (jnp.asarray(selq, BF16), jnp.asarray(selk, BF16), jnp.asarray(oneq), jnp.asarray(onek))


def _pre_call(x, mod3, g1, w1, gqa, gka, gqf, gkf, bf_pad, positions):
    bsz, seq, _ = x.shape
    tm = TM_PRE
    half = HEAD_DIM // 2
    inv_freq = ROPE_THETA ** (-jnp.arange(half, dtype=F32) / half)
    invf = jnp.concatenate([inv_freq, inv_freq, jnp.zeros((HEAD_PAD - HEAD_DIM,), F32)]).reshape(1, HEAD_PAD)
    sign = jnp.concatenate([-jnp.ones((half,), F32), jnp.ones((half,), F32),
                            jnp.zeros((HEAD_PAD - HEAD_DIM,), F32)]).reshape(1, HEAD_PAD)
    tri = jnp.asarray(np.tril(np.ones((tm, tm), np.float32)), BF16)
    selq, selk, oneq, onek = _decay_selectors()

    def full(a):
        return pl.BlockSpec(a.shape, lambda b, s: (0,) * a.ndim)

    def tok(width):
        return pl.BlockSpec((1, tm, width), lambda b, s: (b, s, 0))

    consts = (g1, w1, gqa, gka, gqf, gkf, bf_pad)
    tail = (invf, sign, tri, selq, selk, oneq, onek)
    out_w = (SWA_Q_HEADS * HEAD_PAD, SWA_KV_HEADS * HEAD_PAD, SWA_KV_W,
             FOX_HEADS * HEAD_PAD, FOX_HEADS * HEAD_PAD)
    assert tm == TQ_FOX
    return pl.pallas_call(
        _pre_kernel,
        out_shape=[jax.ShapeDtypeStruct((bsz, seq, w), BF16) for w in out_w]
        + [jax.ShapeDtypeStruct((bsz, seq // tm, FOX_W, tm), BF16)],
        grid=(bsz, seq // tm),
        in_specs=[tok(D_MODEL), pl.BlockSpec((1, N_MOD, D_MODEL), lambda b, s: (b, 0, 0))]
        + [full(a) for a in consts] + [tok(1)] + [full(a) for a in tail],
        out_specs=[tok(w) for w in out_w]
        + [pl.BlockSpec((1, 1, FOX_W, tm), lambda b, s: (b, s, 0, 0))],
        scratch_shapes=[pltpu.VMEM((1, HEAD_PAD), F32)],
        compiler_params=pltpu.CompilerParams(
            dimension_semantics=("parallel", "arbitrary"), vmem_limit_bytes=VMEM_LIMIT),
        name="pre",
    )(x, mod3, *consts, positions.reshape(bsz, seq, 1), *tail)


def _swa_kernel(sink_ref, q_ref, kc_ref, kp_ref, vc_ref, vp_ref, o_ref):
    t_idx = pl.program_id(1)
    tq = q_ref.shape[1]
    nsub = tq // BLOCK
    k_all = jnp.concatenate([kp_ref[0], kc_ref[0]], axis=0)
    v_all = jnp.concatenate([vp_ref[0], vc_ref[0]], axis=0)
    v_swap = jnp.concatenate([v_all[:, HEAD_DIM:], v_all[:, :HEAD_DIM]], axis=1)

    rows = SWA_GROUP * BLOCK
    r = lax.broadcasted_iota(jnp.int32, (rows, 2 * BLOCK), 0)
    kcol = lax.broadcasted_iota(jnp.int32, (rows, 2 * BLOCK), 1)
    rel = (r % BLOCK) + BLOCK - kcol
    in_window = (rel >= 0) & (rel < SWA_WINDOW)
    rgrp = lax.broadcasted_iota(jnp.int32, (rows, 1), 0) // BLOCK
    lane_lo = lax.broadcasted_iota(jnp.int32, (1, HEAD_PAD), 1) < HEAD_DIM

    for j in range(nsub):
        lo_key = jnp.where(t_idx == 0, BLOCK, 0) if j == 0 else 0
        mask = in_window & (kcol >= lo_key)
        vb = v_all[j * BLOCK:(j + 2) * BLOCK, :]
        vsb = v_swap[j * BLOCK:(j + 2) * BLOCK, :]
        for g in range(SWA_KV_HEADS):
            kb = k_all[j * BLOCK:(j + 2) * BLOCK, g * HEAD_PAD:(g + 1) * HEAD_PAD]
            qg = jnp.concatenate(
                [q_ref[0, j * BLOCK:(j + 1) * BLOCK,
                       (g * SWA_GROUP + u) * HEAD_PAD:(g * SWA_GROUP + u + 1) * HEAD_PAD]
                 for u in range(SWA_GROUP)], axis=0)
            s = jnp.where(mask, _dot_nt(qg, kb), NEG_INF)
            sink = jnp.zeros((rows, 1), F32)
            for u in range(SWA_GROUP):
                sink = jnp.where(rgrp == u, sink_ref[g * SWA_GROUP + u], sink)
            m = jnp.maximum(jnp.max(s, axis=-1, keepdims=True), sink)
            p = jnp.exp(s - m)
            denom = jnp.sum(p, axis=-1, keepdims=True) + jnp.exp(sink - m)
            pb = p.astype(BF16)
            inv = 1.0 / denom
            o_nat = _dot(pb, vb) * inv
            o_swp = _dot(pb, vsb) * inv
            o_lo, o_hi = (o_nat, o_swp) if g == 0 else (o_swp, o_nat)
            for w in range(SWA_GROUP // 2):
                ev = o_lo[(2 * w) * BLOCK:(2 * w + 1) * BLOCK, :]
                od = o_hi[(2 * w + 1) * BLOCK:(2 * w + 2) * BLOCK, :]
                pc = g * (SWA_GROUP // 2) + w
                o_ref[0, j * BLOCK:(j + 1) * BLOCK, pc * HEAD_PAD:(pc + 1) * HEAD_PAD] = (
                    jnp.where(lane_lo, ev, od).astype(BF16))


def _swa_call(qa, ka, va, sinks):
    bsz, seq, _ = qa.shape
    tq = TQ_SWA
    per = tq // BLOCK

    def cur(width):
        return pl.BlockSpec((1, tq, width), lambda b, t: (b, t, 0))

    def prev(width):
        return pl.BlockSpec((1, BLOCK, width), lambda b, t: (b, jnp.maximum(t * per - 1, 0), 0))

    return pl.pallas_call(
        _swa_kernel,
        out_shape=jax.ShapeDtypeStruct((bsz, seq, SWA_Q_W), BF16),
        grid=(bsz, seq // tq),
        in_specs=[pl.BlockSpec(memory_space=pltpu.SMEM),
                  cur(SWA_Q_HEADS * HEAD_PAD),
                  cur(SWA_KV_HEADS * HEAD_PAD), prev(SWA_KV_HEADS * HEAD_PAD),
                  cur(SWA_KV_W), prev(SWA_KV_W)],
        out_specs=cur(SWA_Q_W),
        compiler_params=pltpu.CompilerParams(
            dimension_semantics=("parallel", "parallel"), vmem_limit_bytes=VMEM_LIMIT),
        name="swa",
    )(sinks.astype(F32), qa, ka, ka, va, va)


def _fox_kernel(q_ref, k_ref, vt_ref, o_ref):
    qi = pl.program_id(2)
    tq = q_ref.shape[1]
    tk = tq
    nq = tq // FOX_CHAIN_Q
    key = lax.broadcasted_iota(jnp.int32, (tk, FOX_CHAIN_Q), 0)
    qry = lax.broadcasted_iota(jnp.int32, (tk, FOX_CHAIN_Q), 1)

    def step(j, carry, masked):
        start = pl.multiple_of(j * tk, tk)
        chains = [(e, c) for e in range(2) for c in range(nq)]

        def scores(e, c):
            k = k_ref[0, pl.ds(start, tk), e * HEAD_PAD:(e + 1) * HEAD_PAD]
            q = q_ref[0, c * FOX_CHAIN_Q:(c + 1) * FOX_CHAIN_Q, e * HEAD_PAD:(e + 1) * HEAD_PAD]
            s = _dot_nt(k, q)
            if masked:
                s = jnp.where(key <= qry + c * FOX_CHAIN_Q, s, NEG_INF)
            return s

        out = []
        ready = [scores(*ch) for ch in chains[:FOX_AHEAD]]
        pending = None
        for idx, (e, c) in enumerate(chains):
            s = ready.pop(0)
            if idx + FOX_AHEAD < len(chains):
                ready.append(scores(*chains[idx + FOX_AHEAD]))
            m, l, acc = carry[idx]
            m_new = jnp.maximum(m, jnp.max(s, axis=0, keepdims=True))
            alpha = jnp.exp(m - m_new)
            p = jnp.exp(s - m_new)
            l = alpha * l + jnp.sum(p, axis=0, keepdims=True)
            if pending is not None:
                out.append(pending())
            vt = vt_ref[0, j, e * HEAD_DIM:(e + 1) * HEAD_DIM, :]
            pending = functools.partial(
                lambda m_new, l, alpha, acc, vt, p: (m_new, l, alpha * acc + _dot(vt, p)),
                m_new, l, alpha, acc, vt, p.astype(BF16))
        out.append(pending())
        return tuple(out)

    init1 = (jnp.full((1, FOX_CHAIN_Q), NEG_INF, F32), jnp.zeros((1, FOX_CHAIN_Q), F32),
             jnp.zeros((HEAD_DIM, FOX_CHAIN_Q), F32))
    carry = lax.fori_loop(0, qi, functools.partial(step, masked=False), (init1,) * (2 * nq))
    fin = step(qi, carry, True)
    o_t = jnp.concatenate(
        [jnp.concatenate([fin[e * nq + c][2] * (1.0 / fin[e * nq + c][1]) for c in range(nq)], axis=1)
         for e in range(2)], axis=0)
    o_ref[0] = o_t.T.astype(BF16)


def _fox_call(qf, kf, vft):
    bsz, seq, _ = qf.shape
    tq = TQ_FOX
    pairs = FOX_HEADS // 2
    return pl.pallas_call(
        _fox_kernel,
        out_shape=jax.ShapeDtypeStruct((bsz, seq, FOX_W), BF16),
        grid=(bsz, pairs, seq // tq),
        in_specs=[pl.BlockSpec((1, tq, 2 * HEAD_PAD), lambda b, p, t: (b, t, p)),
                  pl.BlockSpec((1, seq, 2 * HEAD_PAD), lambda b, p, t: (b, 0, p)),
                  pl.BlockSpec((1, seq // tq, 2 * HEAD_DIM, tq), lambda b, p, t: (b, 0, p, 0))],
        out_specs=pl.BlockSpec((1, tq, 2 * HEAD_DIM), lambda b, p, t: (b, t, p)),
        compiler_params=pltpu.CompilerParams(
            dimension_semantics=("parallel", "parallel", "arbitrary"),
            vmem_limit_bytes=VMEM_LIMIT),
        name="fox",
    )(qf, kf, vft)


def _post_kernel(x_ref, mod_ref, g1_ref, wg_ref, oa_ref, ob_ref, wa_ref, wb_ref, wo_ref, g2_ref,
                 x1_ref, h2_ref):
    x = x_ref[0]
    mod = mod_ref[0]
    h = _rms_modulate(x, g1_ref[...], mod[1:2, :], mod[0:1, :]).astype(BF16)
    gate_a = jax.nn.sigmoid(_dot(h, wg_ref[:, :D_MODEL]))
    gate_b = jax.nn.sigmoid(_dot(h, wg_ref[:, D_MODEL:]))
    merged = gate_a * _dot(oa_ref[0], wa_ref[...]) + gate_b * _dot(ob_ref[0], wb_ref[...])
    x1 = x + mod[2:3, :] * _dot(merged.astype(BF16), wo_ref[...])
    x1_ref[0] = x1
    h2_ref[0] = _rms_modulate(x1, g2_ref[...], mod[4:5, :], mod[3:4, :]).astype(BF16)


def _post_call(x, mod3, g1, wg, out_a, out_b, wa, wb, wo, g2):
    bsz, seq, _ = x.shape
    tm = TM_PRE

    def full(a):
        return pl.BlockSpec(a.shape, lambda b, s: (0,) * a.ndim)

    def tok(width):
        return pl.BlockSpec((1, tm, width), lambda b, s: (b, s, 0))

    return pl.pallas_call(
        _post_kernel,
        out_shape=[jax.ShapeDtypeStruct((bsz, seq, D_MODEL), F32),
                   jax.ShapeDtypeStruct((bsz, seq, D_MODEL), BF16)],
        grid=(bsz, seq // tm),
        in_specs=[tok(D_MODEL), pl.BlockSpec((1, N_MOD, D_MODEL), lambda b, s: (b, 0, 0)),
                  full(g1), full(wg), tok(SWA_Q_W), tok(FOX_W), full(wa), full(wb), full(wo),
                  full(g2)],
        out_specs=[tok(D_MODEL), tok(D_MODEL)],
        compiler_params=pltpu.CompilerParams(
            dimension_semantics=("parallel", "parallel"), vmem_limit_bytes=VMEM_LIMIT),
        name="post",
    )(x, mod3, g1, wg, out_a, out_b, wa, wb, wo, g2)


TOPK_MARK = 2.0 ** 100


def _top16(s):
    cur = s
    vals = []
    for r in range(PEER_TOPK):
        mx = jnp.max(cur, axis=0, keepdims=True)
        vals.append(mx)
        cur = jnp.where(cur == mx, -(r + 1) * TOPK_MARK, cur)
    rank = jnp.where(cur <= -0.5 * TOPK_MARK, cur * (-1.0 / TOPK_MARK) - 1.0, float(PEER_TOPK))
    return vals, rank


CAND_LOW_B = (16, 8, 5, 4)
CAND_HIGH_A = (15, 7, 4)


def _route_chunk(s0, s1):
    v0, rank0 = _top16(s0)
    v1, rank1 = _top16(s1)
    v0_all = jnp.concatenate(v0, axis=0)
    v1_all = jnp.concatenate(v1, axis=0)
    idx = lax.broadcasted_iota(jnp.int32, (PEER_TOPK, 1), 0)
    low = [jnp.where(idx < CAND_LOW_B[a], v0[a] + v1_all, -3.0e38) for a in range(len(CAND_LOW_B))]
    high = [jnp.where((idx >= len(CAND_LOW_B)) & (idx <= CAND_HIGH_A[b]), v0_all + v1[b], -3.0e38)
            for b in range(len(CAND_HIGH_A))]
    cand = jnp.concatenate(low + high, axis=0)
    cur = cand
    tau = None
    for r in range(PEER_TOPK):
        tau = jnp.max(cur, axis=0, keepdims=True)
        cur = jnp.where(cur == tau, -3.0e38, cur)
    top = v0[0] + v1[0]
    sel = jnp.where(cand >= tau, 1.0, 0.0)
    z = jnp.sum(sel * jnp.exp(cand - top), axis=0, keepdims=True)
    n_low = len(CAND_LOW_B)
    n_high = sel[n_low * PEER_TOPK:(n_low + 1) * PEER_TOPK, :]
    for b in range(1, len(CAND_HIGH_A)):
        n_high = n_high + sel[(n_low + b) * PEER_TOPK:(n_low + b + 1) * PEER_TOPK, :]
    cnt = jnp.zeros_like(s0)
    for a in range(PEER_TOPK):
        if a < n_low:
            n_a = jnp.sum(sel[a * PEER_TOPK:(a + 1) * PEER_TOPK, :], axis=0, keepdims=True)
        else:
            n_a = n_high[a:a + 1, :]
        cnt = jnp.where(rank0 == float(a), n_a, cnt)
    return cnt, jnp.exp(s0 - v0[0]) * (0.5 / z), rank1, jnp.exp(s1 - v1[0])


def _route_kernel(h_ref, wq_ref, keys_ref, n_ref, c0_ref, r1_ref, e1_ref, qt_ref, s_ref):
    qt_ref[...] = _dot_nt(wq_ref[...], h_ref[...]).astype(BF16)
    tt = h_ref.shape[0]

    def head(hh, _):
        base = pl.multiple_of(hh * PEER_QDIM, PEER_QDIM)
        s_ref[0] = _dot(keys_ref[2 * hh], qt_ref[pl.ds(base, PEER_HALF), :])
        s_ref[1] = _dot(keys_ref[2 * hh + 1], qt_ref[pl.ds(base + PEER_HALF, PEER_HALF), :])
        rows = pl.ds(pl.multiple_of(hh * PEER_NKEYS, PEER_NKEYS), PEER_NKEYS)
        half = pl.ds(pl.multiple_of(hh * (PEER_NKEYS // 2), PEER_NKEYS // 2), PEER_NKEYS // 2)
        parts = [_route_chunk(s_ref[0, :, lc * 128:(lc + 1) * 128],
                              s_ref[1, :, lc * 128:(lc + 1) * 128]) for lc in range(tt // 128)]
        cnt, c0, rank1, e1 = (jnp.concatenate(t, axis=1) for t in zip(*parts))
        n_ref[rows, :] = cnt
        c0_ref[rows, :] = c0
        r1_ref[half, :] = pltpu.bitcast(rank1.astype(BF16), jnp.uint32)
        e1_ref[half, :] = pltpu.bitcast(e1.astype(BF16), jnp.uint32)
        return 0

    lax.fori_loop(0, PEER_HEADS, head, 0)


def _route_call(h2, wq_t, keys):
    t_all = h2.shape[0]
    tt = TT_ROUTE
    rows = PEER_HEADS * PEER_NKEYS
    out = jax.ShapeDtypeStruct((rows, t_all), F32)
    out_b = jax.ShapeDtypeStruct((rows // 2, t_all), jnp.uint32)
    return pl.pallas_call(
        _route_kernel,
        out_shape=[out, out, out_b, out_b],
        grid=(t_all // tt,),
        in_specs=[pl.BlockSpec((tt, D_MODEL), lambda t: (t, 0)),
                  pl.BlockSpec(wq_t.shape, lambda t: (0, 0)),
                  pl.BlockSpec(keys.shape, lambda t: (0, 0, 0))],
        out_specs=[pl.BlockSpec((rows, tt), lambda t: (0, t))] * 2
        + [pl.BlockSpec((rows // 2, tt), lambda t: (0, t))] * 2,
        scratch_shapes=[pltpu.VMEM((PEER_HEADS * PEER_QDIM, tt), BF16),
                        pltpu.VMEM((2, PEER_NKEYS, tt), F32)],
        compiler_params=pltpu.CompilerParams(
            dimension_semantics=("parallel",), vmem_limit_bytes=VMEM_LIMIT),
        name="route",
    )(h2, wq_t, keys)


def _peer_kernel(h_ref, u_ref, vt_ref, n_ref, c0_ref, r1_ref, e1_ref, x1_ref, mod_ref,
                 o_ref, acc_ref, nsel_ref, csel_ref, a_ref, w_ref):
    e_idx = pl.program_id(1)
    te, tt = a_ref.shape
    blocks = te // PEER_NKEYS
    assert blocks == 8
    tile3 = (PEER_NKEYS // BF16_ROWS, BF16_ROWS, 128)

    @pl.when(e_idx == 0)
    def _():
        acc_ref[...] = jnp.zeros_like(acc_ref)

    for hh in range(PEER_HEADS):
        grp = pl.ds(pl.multiple_of(hh * PEER_NKEYS + e_idx * blocks, 8), 8)
        nsel_ref[hh * 8:(hh + 1) * 8, :] = n_ref[grp, :]
        csel_ref[hh * 8:(hh + 1) * 8, :] = c0_ref[grp, :]

    a_ref[...] = _dot_nt(u_ref[...], h_ref[...])
    for ii in range(blocks):
        rows = slice(ii * PEER_NKEYS, (ii + 1) * PEER_NKEYS)
        for lc in range(tt // 128):
            lanes = slice(lc * 128, (lc + 1) * 128)
            p = None
            for hh in range(PEER_HEADS):
                n_row = nsel_ref[hh * 8 + ii:hh * 8 + ii + 1, lanes]
                c_row = csel_ref[hh * 8 + ii:hh * 8 + ii + 1, lanes]
                n_b = jnp.broadcast_to(n_row, (BF16_ROWS, 128)).astype(BF16)[None]
                c_b = jnp.broadcast_to(c_row, (BF16_ROWS, 128)).astype(BF16)[None]
                words = slice(hh * (PEER_NKEYS // 2), (hh + 1) * (PEER_NKEYS // 2))
                r1 = pltpu.bitcast(r1_ref[words, lanes], BF16).reshape(tile3)
                e1 = pltpu.bitcast(e1_ref[words, lanes], BF16).reshape(tile3)
                term = jnp.where(r1 < n_b, e1 * c_b, jnp.zeros((), BF16))
                p = term if p is None else p + term
            a = a_ref[rows, lanes]
            gelu2 = a * (1.0 + lax.erf(a * (2.0 ** -0.5)))
            w_ref[rows, lanes] = p.reshape(PEER_NKEYS, 128) * gelu2.astype(BF16)
    acc_ref[...] += _dot(vt_ref[0], w_ref[...])

    @pl.when(e_idx == pl.num_programs(1) - 1)
    def _():
        o_ref[...] = x1_ref[...] + mod_ref[0][5:6, :] * acc_ref[...].T


def _peer_call(h2, u_b, vt_b, n_t, c0_t, r1_t, e1_t, x1, mod3, tiles_per_batch):
    t_all = h2.shape[0]
    tt, te = TT_PEER, TE_PEER
    rows = PEER_HEADS * PEER_NKEYS

    def tab(nrows=rows):
        return pl.BlockSpec((nrows, tt), lambda t, e: (0, t))

    return pl.pallas_call(
        _peer_kernel,
        out_shape=jax.ShapeDtypeStruct((t_all, D_MODEL), F32),
        grid=(t_all // tt, PEER_EXPERTS // te),
        in_specs=[pl.BlockSpec((tt, D_MODEL), lambda t, e: (t, 0)),
                  pl.BlockSpec((te, D_MODEL), lambda t, e: (e, 0)),
                  pl.BlockSpec((1, D_MODEL, te), lambda t, e: (e, 0, 0)),
                  tab(), tab(), tab(rows // 2), tab(rows // 2),
                  pl.BlockSpec((tt, D_MODEL), lambda t, e: (t, 0)),
                  pl.BlockSpec((1, N_MOD, D_MODEL), lambda t, e: (t // tiles_per_batch, 0, 0))],
        out_specs=pl.BlockSpec((tt, D_MODEL), lambda t, e: (t, 0)),
        scratch_shapes=[pltpu.VMEM((D_MODEL, tt), F32),
                        pltpu.VMEM((PEER_HEADS * 8, tt), F32), pltpu.VMEM((PEER_HEADS * 8, tt), F32),
                        pltpu.VMEM((te, tt), F32), pltpu.VMEM((te, tt), BF16)],
        compiler_params=pltpu.CompilerParams(
            dimension_semantics=("parallel", "arbitrary"), vmem_limit_bytes=VMEM_LIMIT),
        name="peer",
    )(h2, u_b, vt_b, n_t, c0_t, r1_t, e1_t, x1, mod3)


def _layer(x, c, positions, w_mod, b_mod, norm1_g, w_in, q_norm_swa, k_norm_swa, sinks,
           q_norm_fox, k_norm_fox, b_forget, w_out_swa, w_out_fox, w_o, norm2_g,
           peer_w_query, peer_sub_keys, peer_u, peer_v):
    bsz, seq, _ = x.shape
    split_at = [int(v) for v in np.cumsum(IN_SPLITS)[:-1]]
    w_qa, w_ka, w_va, w_qf, w_kf, w_vf, w_f, w_ga, w_gb = jnp.split(w_in, split_at, axis=1)
    w1 = jnp.concatenate(
        [_pad_heads_cols(w_qa, SWA_Q_HEADS), _pad_heads_cols(w_ka, SWA_KV_HEADS), w_va,
         jnp.pad(w_f, ((0, 0), (0, HEAD_PAD - FOX_HEADS))),
         _pad_heads_cols(w_qf, FOX_HEADS), _pad_heads_cols(w_kf, FOX_HEADS), w_vf],
        axis=1).astype(BF16)
    wg = jnp.concatenate([w_ga, w_gb], axis=1).astype(BF16)
    bf_pad = jnp.pad(b_forget.astype(F32), (0, HEAD_PAD - FOX_HEADS)).reshape(1, HEAD_PAD)
    g1 = norm1_g.astype(F32).reshape(1, D_MODEL)
    g2 = norm2_g.astype(F32).reshape(1, D_MODEL)

    mod3 = _mod_call(c, w_mod, b_mod).reshape(bsz, N_MOD, D_MODEL)

    qa, ka, va, qf, kf, vf = _pre_call(
        x, mod3, g1, w1,
        _pad_head_gain(q_norm_swa, SWA_Q_HEADS), _pad_head_gain(k_norm_swa, SWA_KV_HEADS),
        _pad_head_gain(q_norm_fox, FOX_HEADS), _pad_head_gain(k_norm_fox, FOX_HEADS),
        bf_pad, positions)

    out_a = _swa_call(qa, ka, va, sinks)
    out_b = _fox_call(qf, kf, vf)

    x1, h2 = _post_call(x, mod3, g1, wg, out_a, out_b, w_out_swa.astype(BF16),
                        w_out_fox.astype(BF16), w_o.astype(BF16), g2)

    t_all = bsz * seq
    h2f = h2.reshape(t_all, D_MODEL)
    keys = peer_sub_keys.reshape(PEER_HEADS * 2, PEER_NKEYS, PEER_HALF).astype(BF16)
    n_t, c0_t, r1_t, e1_t = _route_call(h2f, peer_w_query.T.astype(BF16), keys)
    vt_tiles = peer_v.astype(BF16).reshape(PEER_EXPERTS // TE_PEER, TE_PEER, D_MODEL)
    vt_tiles = vt_tiles.transpose(0, 2, 1)
    out = _peer_call(h2f, peer_u.astype(BF16), vt_tiles, n_t, c0_t, r1_t, e1_t,
                     x1.reshape(t_all, D_MODEL), mod3, seq // TT_PEER)
    return out.reshape(bsz, seq, D_MODEL)


def kernel(x, c, positions, w_mod, b_mod, norm1_g, w_in, q_norm_swa, k_norm_swa, sinks, q_norm_fox, k_norm_fox, b_forget, w_out_swa, w_out_fox, w_o, norm2_g, peer_w_query, peer_sub_keys, peer_u, peer_v):
    for l in range(w_mod.shape[0]):
        x = _layer(x, c, positions, w_mod[l], b_mod[l], norm1_g[l], w_in[l], q_norm_swa[l],
                   k_norm_swa[l], sinks[l], q_norm_fox[l], k_norm_fox[l], b_forget[l],
                   w_out_swa[l], w_out_fox[l], w_o[l], norm2_g[l], peer_w_query[l],
                   peer_sub_keys[l], peer_u[l], peer_v[l])
    return x
```

```python
import functools

import jax
import jax.numpy as jnp
import numpy as np
from jax import lax
from jax.experimental import pallas as pl
from jax.experimental.pallas import tpu as pltpu

D_MODEL = 1024
HEAD_DIM = 64
HEAD_PAD = 128
SWA_Q_HEADS = 8
SWA_KV_HEADS = 2
SWA_GROUP = SWA_Q_HEADS // SWA_KV_HEADS
SWA_WINDOW = 128
FOX_HEADS = 8
BLOCK = 128
ROPE_THETA = 10000.0
PEER_HEADS = 8
PEER_NKEYS = 128
PEER_EXPERTS = PEER_NKEYS * PEER_NKEYS
PEER_QDIM = 256
PEER_HALF = PEER_QDIM // 2
PEER_TOPK = 16
N_MOD = 6
EPS = 1e-6
NEG_INF = -1e30

SWA_Q_W = SWA_Q_HEADS * HEAD_DIM
SWA_KV_W = SWA_KV_HEADS * HEAD_DIM
FOX_W = FOX_HEADS * HEAD_DIM
IN_SPLITS = (SWA_Q_W, SWA_KV_W, SWA_KV_W, FOX_W, FOX_W, FOX_W, FOX_HEADS, D_MODEL, D_MODEL)

C_QA = 0
C_KA = C_QA + SWA_Q_HEADS * HEAD_PAD
C_VA = C_KA + SWA_KV_HEADS * HEAD_PAD
C_F = C_VA + SWA_KV_W
C_QF = C_F + HEAD_PAD
C_KF = C_QF + FOX_HEADS * HEAD_PAD
C_VF = C_KF + FOX_HEADS * HEAD_PAD
C_END = C_VF + FOX_W
PROJ_W = 256

TM_PRE = 512
TQ_SWA = 512
SWA_AHEAD = 2
TQ_FOX = 512
FOX_CHAIN_Q = 256
FOX_AHEAD = 2
TT_ROUTE = 256
TT_PEER = 512
TE_PEER = 1024
VMEM_LIMIT = 56 * 1024 * 1024

BF16 = jnp.bfloat16
F32 = jnp.float32
BF16_ROWS = 16


def _split3(v):
    hi = v.astype(BF16)
    r1 = v - hi.astype(F32)
    mid = r1.astype(BF16)
    lo = (r1 - mid.astype(F32)).astype(BF16)
    return hi, mid, lo


def _dot_nt(a, b):
    return lax.dot_general(a, b, (((1,), (1,)), ((), ())), preferred_element_type=F32)


def _dot(a, b):
    return jnp.dot(a, b, preferred_element_type=F32)


def _rms_modulate(x, g, scale, shift):
    y = x * lax.rsqrt(jnp.mean(x * x, axis=-1, keepdims=True) + EPS)
    return (y * g) * (1.0 + scale) + shift


def _mod_kernel(c_ref, w_ref, b_ref, o_ref):
    c = c_ref[...]
    a = c * jax.nn.sigmoid(c)
    a_hi, a_mid, a_lo = _split3(a)
    w = w_ref[...]
    w_hi, w_mid, w_lo = _split3(w)
    acc = _dot(a_hi, w_hi) + (_dot(a_hi, w_mid) + _dot(a_mid, w_hi))
    acc = acc + (_dot(a_mid, w_mid) + _dot(a_hi, w_lo) + _dot(a_lo, w_hi))
    o_ref[...] = acc + b_ref[...]


def _mod_call(c, w_mod, b_mod):
    bsz = c.shape[0]
    n = w_mod.shape[1]
    tn = 1024
    return pl.pallas_call(
        _mod_kernel,
        out_shape=jax.ShapeDtypeStruct((bsz, n), F32),
        grid=(n // tn,),
        in_specs=[
            pl.BlockSpec((bsz, D_MODEL), lambda j: (0, 0)),
            pl.BlockSpec((D_MODEL, tn), lambda j: (0, j)),
            pl.BlockSpec((1, tn), lambda j: (0, j)),
        ],
        out_specs=pl.BlockSpec((bsz, tn), lambda j: (0, j)),
        compiler_params=pltpu.CompilerParams(
            dimension_semantics=("parallel",), vmem_limit_bytes=VMEM_LIMIT),
        name="mod",
    )(c, w_mod, b_mod.reshape(1, n))


def _pre_kernel(x_ref, mod_ref, g1_ref, w_ref, gqa_ref, gka_ref, gqf_ref, gkf_ref, bf_ref,
                pos_ref, invf_ref, sign_ref, tri_ref, selq_ref, selk_ref, oneq_ref, onek_ref,
                qa_ref, ka_ref, va_ref, qf_ref, kf_ref, vf_ref, carry_ref):
    s_idx = pl.program_id(1)

    @pl.when(s_idx == 0)
    def _():
        carry_ref[...] = jnp.zeros_like(carry_ref)

    x = x_ref[0]
    mod = mod_ref[0]
    h = _rms_modulate(x, g1_ref[...], mod[1:2, :], mod[0:1, :]).astype(BF16)

    ang = pos_ref[0].astype(F32) * invf_ref[...]
    cos = jnp.cos(ang)
    sin_s = jnp.sin(ang) * sign_ref[...]
    lane = lax.broadcasted_iota(jnp.int32, (1, HEAD_PAD), 1)
    low_half = lane < HEAD_DIM // 2

    def head_norm(c, g):
        return c * lax.rsqrt(jnp.sum(c * c, axis=-1, keepdims=True) * (1.0 / HEAD_DIM) + EPS) * g

    def rope(c):
        partner = jnp.where(low_half, pltpu.roll(c, HEAD_PAD - HEAD_DIM // 2, 1),
                            pltpu.roll(c, HEAD_DIM // 2, 1))
        return c * cos + partner * sin_s

    scale = HEAD_DIM ** -0.5

    def proj_chunks(c0, width):
        for j in range(width // PROJ_W):
            wide = _dot(h, w_ref[:, c0 + j * PROJ_W:c0 + (j + 1) * PROJ_W])
            for i in range(PROJ_W // HEAD_PAD):
                yield wide[:, i * HEAD_PAD:(i + 1) * HEAD_PAD]

    for hh, c in enumerate(proj_chunks(C_QA, SWA_Q_HEADS * HEAD_PAD)):
        c = rope(head_norm(c, gqa_ref[:, hh * HEAD_PAD:(hh + 1) * HEAD_PAD])) * scale
        qa_ref[0, :, hh * HEAD_PAD:(hh + 1) * HEAD_PAD] = c.astype(BF16)
    for hh, c in enumerate(proj_chunks(C_KA, SWA_KV_HEADS * HEAD_PAD)):
        c = rope(head_norm(c, gka_ref[:, hh * HEAD_PAD:(hh + 1) * HEAD_PAD]))
        ka_ref[0, :, hh * HEAD_PAD:(hh + 1) * HEAD_PAD] = c.astype(BF16)
    va, f_logit = proj_chunks(C_VA, SWA_KV_W + HEAD_PAD)
    va_ref[0] = va.astype(BF16)
    for j, c in enumerate(proj_chunks(C_VF, FOX_W)):
        vf_ref[0, 0, j * HEAD_PAD:(j + 1) * HEAD_PAD, :] = c.T.astype(BF16)

    f_logit = f_logit + bf_ref[...]
    log_f = jnp.minimum(f_logit, 0.0) - jnp.log1p(jnp.exp(-jnp.abs(f_logit)))
    l_hi, l_mid, l_lo = _split3(log_f)
    tri = tri_ref[...]
    dcum = (_dot(tri, l_hi) + _dot(tri, l_mid) + _dot(tri, l_lo)) + carry_ref[...]
    tm = dcum.shape[0]
    carry_ref[...] = dcum[tm - 1:tm, :]

    d_hi, d_mid, d_lo = _split3(dcum)
    d3 = jnp.concatenate([d_hi, d_mid, d_lo], axis=1)
    aug_q = _dot(d3, selq_ref[...]) + oneq_ref[...]
    aug_k = onek_ref[...] - _dot(d3, selk_ref[...])

    for hh, c in enumerate(proj_chunks(C_QF, FOX_HEADS * HEAD_PAD)):
        sl = slice(hh * HEAD_PAD, (hh + 1) * HEAD_PAD)
        qf_ref[0, :, sl] = (head_norm(c, gqf_ref[:, sl]) * scale + aug_q[:, sl]).astype(BF16)
    for hh, c in enumerate(proj_chunks(C_KF, FOX_HEADS * HEAD_PAD)):
        sl = slice(hh * HEAD_PAD, (hh + 1) * HEAD_PAD)
        kf_ref[0, :, sl] = (head_norm(c, gkf_ref[:, sl]) + aug_k[:, sl]).astype(BF16)


def _pad_heads_cols(w, nheads):
    k = w.shape[0]
    w = w.reshape(k, nheads, HEAD_DIM)
    w = jnp.pad(w, ((0, 0), (0, 0), (0, HEAD_PAD - HEAD_DIM)))
    return w.reshape(k, nheads * HEAD_PAD)


def _pad_head_gain(g, nheads):
    g = jnp.pad(g.astype(F32), (0, HEAD_PAD - HEAD_DIM))
    return jnp.tile(g, nheads).reshape(1, nheads * HEAD_PAD)


def _decay_selectors():
    selq = np.zeros((3 * HEAD_PAD, FOX_HEADS * HEAD_PAD), np.float32)
    selk = np.zeros((3 * HEAD_PAD, FOX_HEADS * HEAD_PAD), np.float32)
    oneq = np.zeros((1, FOX_HEADS * HEAD_PAD), np.float32)
    onek = np.zeros((1, FOX_HEADS * HEAD_PAD), np.float32)
    for hh in range(FOX_HEADS):
        for p in range(3):
            selq[p * HEAD_PAD + hh, hh * HEAD_PAD + HEAD_DIM + p] = 1.0
            selk[p * HEAD_PAD + hh, hh * HEAD_PAD + HEAD_DIM + 3 + p] = 1.0
            oneq[0, hh * HEAD_PAD + HEAD_DIM + 3 + p] = 1.0
            onek[0, hh * HEAD_PAD + HEAD_DIM + p] = 1.0
    return (jnp.asarray(selq, BF16), jnp.asarray(selk, BF16), jnp.asarray(oneq), jnp.asarray(onek))


def _pre_call(x, mod3, g1, w1, gqa, gka, gqf, gkf, bf_pad, positions):
    bsz, seq, _ = x.shape
    tm = TM_PRE
    half = HEAD_DIM // 2
    inv_freq = ROPE_THETA ** (-jnp.arange(half, dtype=F32) / half)
    invf = jnp.concatenate([inv_freq, inv_freq, jnp.zeros((HEAD_PAD - HEAD_DIM,), F32)]).reshape(1, HEAD_PAD)
    sign = jnp.concatenate([-jnp.ones((half,), F32), jnp.ones((half,), F32),
                            jnp.zeros((HEAD_PAD - HEAD_DIM,), F32)]).reshape(1, HEAD_PAD)
    tri = jnp.asarray(np.tril(np.ones((tm, tm), np.float32)), BF16)
    selq, selk, oneq, onek = _decay_selectors()

    def full(a):
        return pl.BlockSpec(a.shape, lambda b, s: (0,) * a.ndim)

    def tok(width):
        return pl.BlockSpec((1, tm, width), lambda b, s: (b, s, 0))

    consts = (g1, w1, gqa, gka, gqf, gkf, bf_pad)
    tail = (invf, sign, tri, selq, selk, oneq, onek)
    out_w = (SWA_Q_HEADS * HEAD_PAD, SWA_KV_HEADS * HEAD_PAD, SWA_KV_W,
             FOX_HEADS * HEAD_PAD, FOX_HEADS * HEAD_PAD)
    assert tm == TQ_FOX
    return pl.pallas_call(
        _pre_kernel,
        out_shape=[jax.ShapeDtypeStruct((bsz, seq, w), BF16) for w in out_w]
        + [jax.ShapeDtypeStruct((bsz, seq // tm, FOX_W, tm), BF16)],
        grid=(bsz, seq // tm),
        in_specs=[tok(D_MODEL), pl.BlockSpec((1, N_MOD, D_MODEL), lambda b, s: (b, 0, 0))]
        + [full(a) for a in consts] + [tok(1)] + [full(a) for a in tail],
        out_specs=[tok(w) for w in out_w]
        + [pl.BlockSpec((1, 1, FOX_W, tm), lambda b, s: (b, s, 0, 0))],
        scratch_shapes=[pltpu.VMEM((1, HEAD_PAD), F32)],
        compiler_params=pltpu.CompilerParams(
            dimension_semantics=("parallel", "arbitrary"), vmem_limit_bytes=VMEM_LIMIT),
        name="pre",
    )(x, mod3, *consts, positions.reshape(bsz, seq, 1), *tail)


def _swa_kernel(sink_ref, q_ref, kc_ref, kp_ref, vc_ref, vp_ref, o_ref):
    t_idx = pl.program_id(1)
    tq = q_ref.shape[1]
    nsub = tq // BLOCK
    k_all = jnp.concatenate([kp_ref[0], kc_ref[0]], axis=0)
    v_all = jnp.concatenate([vp_ref[0], vc_ref[0]], axis=0)
    v_swap = jnp.concatenate([v_all[:, HEAD_DIM:], v_all[:, :HEAD_DIM]], axis=1)

    rows = SWA_GROUP * BLOCK
    r = lax.broadcasted_iota(jnp.int32, (rows, 2 * BLOCK), 0)
    kcol = lax.broadcasted_iota(jnp.int32, (rows, 2 * BLOCK), 1)
    rel = (r % BLOCK) + BLOCK - kcol
    in_window = (rel >= 0) & (rel < SWA_WINDOW)
    rgrp = lax.broadcasted_iota(jnp.int32, (rows, 1), 0) // BLOCK
    lane_lo = lax.broadcasted_iota(jnp.int32, (1, HEAD_PAD), 1) < HEAD_DIM

    order = tuple(range(0, SWA_GROUP, 2)) + tuple(range(1, SWA_GROUP, 2))
    half = rows // 2
    units = [(j, g) for j in range(nsub) for g in range(SWA_KV_HEADS)]

    def scores(j, g):
        lo_key = jnp.where(t_idx == 0, BLOCK, 0) if j == 0 else 0
        kb = k_all[j * BLOCK:(j + 2) * BLOCK, g * HEAD_PAD:(g + 1) * HEAD_PAD]
        qg = jnp.concatenate(
            [q_ref[0, j * BLOCK:(j + 1) * BLOCK,
                   (g * SWA_GROUP + u) * HEAD_PAD:(g * SWA_GROUP + u + 1) * HEAD_PAD]
             for u in order], axis=0)
        return jnp.where(in_window & (kcol >= lo_key), _dot_nt(qg, kb), NEG_INF)

    ready = [scores(*u) for u in units[:SWA_AHEAD]]
    for idx, (j, g) in enumerate(units):
        s = ready.pop(0)
        if idx + SWA_AHEAD < len(units):
            ready.append(scores(*units[idx + SWA_AHEAD]))
        sink = jnp.zeros((rows, 1), F32)
        for b, u in enumerate(order):
            sink = jnp.where(rgrp == b, sink_ref[g * SWA_GROUP + u], sink)
        m = jnp.maximum(jnp.max(s, axis=-1, keepdims=True), sink)
        p = jnp.exp(s - m)
        inv = 1.0 / (jnp.sum(p, axis=-1, keepdims=True) + jnp.exp(sink - m))
        pb = p.astype(BF16)
        vb = v_all[j * BLOCK:(j + 2) * BLOCK, :]
        vsb = v_swap[j * BLOCK:(j + 2) * BLOCK, :]
        v_even, v_odd = (vb, vsb) if g == 0 else (vsb, vb)
        o_even = _dot(pb[:half], v_even) * inv[:half]
        o_odd = _dot(pb[half:], v_odd) * inv[half:]
        for w in range(SWA_GROUP // 2):
            pc = g * (SWA_GROUP // 2) + w
            o_ref[0, j * BLOCK:(j + 1) * BLOCK, pc * HEAD_PAD:(pc + 1) * HEAD_PAD] = jnp.where(
                lane_lo, o_even[w * BLOCK:(w + 1) * BLOCK, :],
                o_odd[w * BLOCK:(w + 1) * BLOCK, :]).astype(BF16)


def _swa_call(qa, ka, va, sinks):
    bsz, seq, _ = qa.shape
    tq = TQ_SWA
    per = tq // BLOCK

    def cur(width):
        return pl.BlockSpec((1, tq, width), lambda b, t: (b, t, 0))

    def prev(width):
        return pl.BlockSpec((1, BLOCK, width), lambda b, t: (b, jnp.maximum(t * per - 1, 0), 0))

    return pl.pallas_call(
        _swa_kernel,
        out_shape=jax.ShapeDtypeStruct((bsz, seq, SWA_Q_W), BF16),
        grid=(bsz, seq // tq),
        in_specs=[pl.BlockSpec(memory_space=pltpu.SMEM),
                  cur(SWA_Q_HEADS * HEAD_PAD),
                  cur(SWA_KV_HEADS * HEAD_PAD), prev(SWA_KV_HEADS * HEAD_PAD),
                  cur(SWA_KV_W), prev(SWA_KV_W)],
        out_specs=cur(SWA_Q_W),
        compiler_params=pltpu.CompilerParams(
            dimension_semantics=("parallel", "parallel"), vmem_limit_bytes=VMEM_LIMIT),
        name="swa",
    )(sinks.astype(F32), qa, ka, ka, va, va)


def _fox_kernel(q_ref, k_ref, vt_ref, o_ref):
    qi = pl.program_id(2)
    tq = q_ref.shape[1]
    tk = tq
    nq = tq // FOX_CHAIN_Q
    key = lax.broadcasted_iota(jnp.int32, (tk, FOX_CHAIN_Q), 0)
    qry = lax.broadcasted_iota(jnp.int32, (tk, FOX_CHAIN_Q), 1)

    def step(j, carry, masked):
        start = pl.multiple_of(j * tk, tk)
        chains = [(e, c) for e in range(2) for c in range(nq)]

        def scores(e, c):
            k = k_ref[0, pl.ds(start, tk), e * HEAD_PAD:(e + 1) * HEAD_PAD]
            q = q_ref[0, c * FOX_CHAIN_Q:(c + 1) * FOX_CHAIN_Q, e * HEAD_PAD:(e + 1) * HEAD_PAD]
            s = _dot_nt(k, q)
            if masked:
                s = jnp.where(key <= qry + c * FOX_CHAIN_Q, s, NEG_INF)
            return s

        out = []
        ready = [scores(*ch) for ch in chains[:FOX_AHEAD]]
        pending = None
        for idx, (e, c) in enumerate(chains):
            s = ready.pop(0)
            if idx + FOX_AHEAD < len(chains):
                ready.append(scores(*chains[idx + FOX_AHEAD]))
            m, l, acc = carry[idx]
            m_new = jnp.maximum(m, jnp.max(s, axis=0, keepdims=True))
            alpha = jnp.exp(m - m_new)
            p = jnp.exp(s - m_new)
            l = alpha * l + jnp.sum(p, axis=0, keepdims=True)
            if pending is not None:
                out.append(pending())
            vt = vt_ref[0, j, e * HEAD_DIM:(e + 1) * HEAD_DIM, :]
            pending = functools.partial(
                lambda m_new, l, alpha, acc, vt, p: (m_new, l, alpha * acc + _dot(vt, p)),
                m_new, l, alpha, acc, vt, p.astype(BF16))
        out.append(pending())
        return tuple(out)

    init1 = (jnp.full((1, FOX_CHAIN_Q), NEG_INF, F32), jnp.zeros((1, FOX_CHAIN_Q), F32),
             jnp.zeros((HEAD_DIM, FOX_CHAIN_Q), F32))
    carry = lax.fori_loop(0, qi, functools.partial(step, masked=False), (init1,) * (2 * nq))
    fin = step(qi, carry, True)
    o_t = jnp.concatenate(
        [jnp.concatenate([fin[e * nq + c][2] * (1.0 / fin[e * nq + c][1]) for c in range(nq)], axis=1)
         for e in range(2)], axis=0)
    o_ref[0] = o_t.T.astype(BF16)


def _fox_call(qf, kf, vft):
    bsz, seq, _ = qf.shape
    tq = TQ_FOX
    pairs = FOX_HEADS // 2
    return pl.pallas_call(
        _fox_kernel,
        out_shape=jax.ShapeDtypeStruct((bsz, seq, FOX_W), BF16),
        grid=(bsz, pairs, seq // tq),
        in_specs=[pl.BlockSpec((1, tq, 2 * HEAD_PAD), lambda b, p, t: (b, t, p)),
                  pl.BlockSpec((1, seq, 2 * HEAD_PAD), lambda b, p, t: (b, 0, p)),
                  pl.BlockSpec((1, seq // tq, 2 * HEAD_DIM, tq), lambda b, p, t: (b, 0, p, 0))],
        out_specs=pl.BlockSpec((1, tq, 2 * HEAD_DIM), lambda b, p, t: (b, t, p)),
        compiler_params=pltpu.CompilerParams(
            dimension_semantics=("parallel", "parallel", "arbitrary"),
            vmem_limit_bytes=VMEM_LIMIT),
        name="fox",
    )(qf, kf, vft)


def _post_kernel(x_ref, mod_ref, g1_ref, wg_ref, oa_ref, ob_ref, wa_ref, wb_ref, wo_ref, g2_ref,
                 x1_ref, h2_ref):
    x = x_ref[0]
    mod = mod_ref[0]
    h = _rms_modulate(x, g1_ref[...], mod[1:2, :], mod[0:1, :]).astype(BF16)
    gate_a = jax.nn.sigmoid(_dot(h, wg_ref[:, :D_MODEL]))
    gate_b = jax.nn.sigmoid(_dot(h, wg_ref[:, D_MODEL:]))
    merged = gate_a * _dot(oa_ref[0], wa_ref[...]) + gate_b * _dot(ob_ref[0], wb_ref[...])
    x1 = x + mod[2:3, :] * _dot(merged.astype(BF16), wo_ref[...])
    x1_ref[0] = x1
    h2_ref[0] = _rms_modulate(x1, g2_ref[...], mod[4:5, :], mod[3:4, :]).astype(BF16)


def _post_call(x, mod3, g1, wg, out_a, out_b, wa, wb, wo, g2):
    bsz, seq, _ = x.shape
    tm = TM_PRE

    def full(a):
        return pl.BlockSpec(a.shape, lambda b, s: (0,) * a.ndim)

    def tok(width):
        return pl.BlockSpec((1, tm, width), lambda b, s: (b, s, 0))

    return pl.pallas_call(
        _post_kernel,
        out_shape=[jax.ShapeDtypeStruct((bsz, seq, D_MODEL), F32),
                   jax.ShapeDtypeStruct((bsz, seq, D_MODEL), BF16)],
        grid=(bsz, seq // tm),
        in_specs=[tok(D_MODEL), pl.BlockSpec((1, N_MOD, D_MODEL), lambda b, s: (b, 0, 0)),
                  full(g1), full(wg), tok(SWA_Q_W), tok(FOX_W), full(wa), full(wb), full(wo),
                  full(g2)],
        out_specs=[tok(D_MODEL), tok(D_MODEL)],
        compiler_params=pltpu.CompilerParams(
            dimension_semantics=("parallel", "parallel"), vmem_limit_bytes=VMEM_LIMIT),
        name="post",
    )(x, mod3, g1, wg, out_a, out_b, wa, wb, wo, g2)


TOPK_MARK = 2.0 ** 100


def _top16(s):
    cur = s
    vals = []
    for r in range(PEER_TOPK):
        mx = jnp.max(cur, axis=0, keepdims=True)
        vals.append(mx)
        cur = jnp.where(cur == mx, -(r + 1) * TOPK_MARK, cur)
    rank = jnp.where(cur <= -0.5 * TOPK_MARK, cur * (-1.0 / TOPK_MARK) - 1.0, float(PEER_TOPK))
    return vals, rank


CAND_LOW_B = (16, 8, 5, 4)
CAND_HIGH_A = (15, 7, 4)


def _route_chunk(s0, s1):
    v0, rank0 = _top16(s0)
    v1, rank1 = _top16(s1)
    v0_all = jnp.concatenate(v0, axis=0)
    v1_all = jnp.concatenate(v1, axis=0)
    idx = lax.broadcasted_iota(jnp.int32, (PEER_TOPK, 1), 0)
    low = [jnp.where(idx < CAND_LOW_B[a], v0[a] + v1_all, -3.0e38) for a in range(len(CAND_LOW_B))]
    high = [jnp.where((idx >= len(CAND_LOW_B)) & (idx <= CAND_HIGH_A[b]), v0_all + v1[b], -3.0e38)
            for b in range(len(CAND_HIGH_A))]
    cand = jnp.concatenate(low + high, axis=0)
    cur = cand
    tau = None
    for r in range(PEER_TOPK):
        tau = jnp.max(cur, axis=0, keepdims=True)
        cur = jnp.where(cur == tau, -3.0e38, cur)
    top = v0[0] + v1[0]
    sel = jnp.where(cand >= tau, 1.0, 0.0)
    z = jnp.sum(sel * jnp.exp(cand - top), axis=0, keepdims=True)
    n_low = len(CAND_LOW_B)
    n_high = sel[n_low * PEER_TOPK:(n_low + 1) * PEER_TOPK, :]
    for b in range(1, len(CAND_HIGH_A)):
        n_high = n_high + sel[(n_low + b) * PEER_TOPK:(n_low + b + 1) * PEER_TOPK, :]
    cnt = jnp.zeros_like(s0)
    for a in range(PEER_TOPK):
        if a < n_low:
            n_a = jnp.sum(sel[a * PEER_TOPK:(a + 1) * PEER_TOPK, :], axis=0, keepdims=True)
        else:
            n_a = n_high[a:a + 1, :]
        cnt = jnp.where(rank0 == float(a), n_a, cnt)
    return cnt, jnp.exp(s0 - v0[0]) * (0.5 / z), rank1, jnp.exp(s1 - v1[0])


def _route_kernel(h_ref, wq_ref, keys_ref, n_ref, c0_ref, r1_ref, e1_ref, qt_ref, s_ref):
    qt_ref[...] = _dot_nt(wq_ref[...], h_ref[...]).astype(BF16)
    tt = h_ref.shape[0]

    def head(hh, _):
        base = pl.multiple_of(hh * PEER_QDIM, PEER_QDIM)
        s_ref[0] = _dot(keys_ref[2 * hh], qt_ref[pl.ds(base, PEER_HALF), :])
        s_ref[1] = _dot(keys_ref[2 * hh + 1], qt_ref[pl.ds(base + PEER_HALF, PEER_HALF), :])
        rows = pl.ds(pl.multiple_of(hh * PEER_NKEYS, PEER_NKEYS), PEER_NKEYS)
        half = pl.ds(pl.multiple_of(hh * (PEER_NKEYS // 2), PEER_NKEYS // 2), PEER_NKEYS // 2)
        parts = [_route_chunk(s_ref[0, :, lc * 128:(lc + 1) * 128],
                              s_ref[1, :, lc * 128:(lc + 1) * 128]) for lc in range(tt // 128)]
        cnt, c0, rank1, e1 = (jnp.concatenate(t, axis=1) for t in zip(*parts))
        n_ref[rows, :] = cnt
        c0_ref[rows, :] = c0
        r1_ref[half, :] = pltpu.bitcast(rank1.astype(BF16), jnp.uint32)
        e1_ref[half, :] = pltpu.bitcast(e1.astype(BF16), jnp.uint32)
        return 0

    lax.fori_loop(0, PEER_HEADS, head, 0)


def _route_call(h2, wq_t, keys):
    t_all = h2.shape[0]
    tt = TT_ROUTE
    rows = PEER_HEADS * PEER_NKEYS
    out = jax.ShapeDtypeStruct((rows, t_all), F32)
    out_b = jax.ShapeDtypeStruct((rows // 2, t_all), jnp.uint32)
    return pl.pallas_call(
        _route_kernel,
        out_shape=[out, out, out_b, out_b],
        grid=(t_all // tt,),
        in_specs=[pl.BlockSpec((tt, D_MODEL), lambda t: (t, 0)),
                  pl.BlockSpec(wq_t.shape, lambda t: (0, 0)),
                  pl.BlockSpec(keys.shape, lambda t: (0, 0, 0))],
        out_specs=[pl.BlockSpec((rows, tt), lambda t: (0, t))] * 2
        + [pl.BlockSpec((rows // 2, tt), lambda t: (0, t))] * 2,
        scratch_shapes=[pltpu.VMEM((PEER_HEADS * PEER_QDIM, tt), BF16),
                        pltpu.VMEM((2, PEER_NKEYS, tt), F32)],
        compiler_params=pltpu.CompilerParams(
            dimension_semantics=("parallel",), vmem_limit_bytes=VMEM_LIMIT),
        name="route",
    )(h2, wq_t, keys)


def _peer_kernel(h_ref, u_ref, vt_ref, n_ref, c0_ref, r1_ref, e1_ref, x1_ref, mod_ref,
                 o_ref, acc_ref, nsel_ref, csel_ref, a_ref, w_ref):
    e_idx = pl.program_id(1)
    te, tt = a_ref.shape
    blocks = te // PEER_NKEYS
    assert blocks == 8
    tile3 = (PEER_NKEYS // BF16_ROWS, BF16_ROWS, 128)

    @pl.when(e_idx == 0)
    def _():
        acc_ref[...] = jnp.zeros_like(acc_ref)

    for hh in range(PEER_HEADS):
        grp = pl.ds(pl.multiple_of(hh * PEER_NKEYS + e_idx * blocks, 8), 8)
        nsel_ref[hh * 8:(hh + 1) * 8, :] = n_ref[grp, :]
        csel_ref[hh * 8:(hh + 1) * 8, :] = c0_ref[grp, :]

    a_ref[...] = _dot_nt(u_ref[...], h_ref[...])
    for ii in range(blocks):
        rows = slice(ii * PEER_NKEYS, (ii + 1) * PEER_NKEYS)
        for lc in range(tt // 128):
            lanes = slice(lc * 128, (lc + 1) * 128)
            p = None
            for hh in range(PEER_HEADS):
                n_row = nsel_ref[hh * 8 + ii:hh * 8 + ii + 1, lanes]
                c_row = csel_ref[hh * 8 + ii:hh * 8 + ii + 1, lanes]
                n_b = jnp.broadcast_to(n_row, (BF16_ROWS, 128)).astype(BF16)[None]
                c_b = jnp.broadcast_to(c_row, (BF16_ROWS, 128)).astype(BF16)[None]
                words = slice(hh * (PEER_NKEYS // 2), (hh + 1) * (PEER_NKEYS // 2))
                r1 = pltpu.bitcast(r1_ref[words, lanes], BF16).reshape(tile3)
                e1 = pltpu.bitcast(e1_ref[words, lanes], BF16).reshape(tile3)
                term = jnp.where(r1 < n_b, e1 * c_b, jnp.zeros((), BF16))
                p = term if p is None else p + term
            a = a_ref[rows, lanes]
            gelu2 = a * (1.0 + lax.erf(a * (2.0 ** -0.5)))
            w_ref[rows, lanes] = p.reshape(PEER_NKEYS, 128) * gelu2.astype(BF16)
    acc_ref[...] += _dot(vt_ref[0], w_ref[...])

    @pl.when(e_idx == pl.num_programs(1) - 1)
    def _():
        o_ref[...] = x1_ref[...] + mod_ref[0][5:6, :] * acc_ref[...].T


def _peer_call(h2, u_b, vt_b, n_t, c0_t, r1_t, e1_t, x1, mod3, tiles_per_batch):
    t_all = h2.shape[0]
    tt, te = TT_PEER, TE_PEER
    rows = PEER_HEADS * PEER_NKEYS

    def tab(nrows=rows):
        return pl.BlockSpec((nrows, tt), lambda t, e: (0, t))

    return pl.pallas_call(
        _peer_kernel,
        out_shape=jax.ShapeDtypeStruct((t_all, D_MODEL), F32),
        grid=(t_all // tt, PEER_EXPERTS // te),
        in_specs=[pl.BlockSpec((tt, D_MODEL), lambda t, e: (t, 0)),
                  pl.BlockSpec((te, D_MODEL), lambda t, e: (e, 0)),
                  pl.BlockSpec((1, D_MODEL, te), lambda t, e: (e, 0, 0)),
                  tab(), tab(), tab(rows // 2), tab(rows // 2),
                  pl.BlockSpec((tt, D_MODEL), lambda t, e: (t, 0)),
                  pl.BlockSpec((1, N_MOD, D_MODEL), lambda t, e: (t // tiles_per_batch, 0, 0))],
        out_specs=pl.BlockSpec((tt, D_MODEL), lambda t, e: (t, 0)),
        scratch_shapes=[pltpu.VMEM((D_MODEL, tt), F32),
                        pltpu.VMEM((PEER_HEADS * 8, tt), F32), pltpu.VMEM((PEER_HEADS * 8, tt), F32),
                        pltpu.VMEM((te, tt), F32), pltpu.VMEM((te, tt), BF16)],
        compiler_params=pltpu.CompilerParams(
            dimension_semantics=("parallel", "arbitrary"), vmem_limit_bytes=VMEM_LIMIT),
        name="peer",
    )(h2, u_b, vt_b, n_t, c0_t, r1_t, e1_t, x1, mod3)


def _layer(x, c, positions, w_mod, b_mod, norm1_g, w_in, q_norm_swa, k_norm_swa, sinks,
           q_norm_fox, k_norm_fox, b_forget, w_out_swa, w_out_fox, w_o, norm2_g,
           peer_w_query, peer_sub_keys, peer_u, peer_v):
    bsz, seq, _ = x.shape
    split_at = [int(v) for v in np.cumsum(IN_SPLITS)[:-1]]
    w_qa, w_ka, w_va, w_qf, w_kf, w_vf, w_f, w_ga, w_gb = jnp.split(w_in, split_at, axis=1)
    w1 = jnp.concatenate(
        [_pad_heads_cols(w_qa, SWA_Q_HEADS), _pad_heads_cols(w_ka, SWA_KV_HEADS), w_va,
         jnp.pad(w_f, ((0, 0), (0, HEAD_PAD - FOX_HEADS))),
         _pad_heads_cols(w_qf, FOX_HEADS), _pad_heads_cols(w_kf, FOX_HEADS), w_vf],
        axis=1).astype(BF16)
    wg = jnp.concatenate([w_ga, w_gb], axis=1).astype(BF16)
    bf_pad = jnp.pad(b_forget.astype(F32), (0, HEAD_PAD - FOX_HEADS)).reshape(1, HEAD_PAD)
    g1 = norm1_g.astype(F32).reshape(1, D_MODEL)
    g2 = norm2_g.astype(F32).reshape(1, D_MODEL)

    mod3 = _mod_call(c, w_mod, b_mod).reshape(bsz, N_MOD, D_MODEL)

    qa, ka, va, qf, kf, vf = _pre_call(
        x, mod3, g1, w1,
        _pad_head_gain(q_norm_swa, SWA_Q_HEADS), _pad_head_gain(k_norm_swa, SWA_KV_HEADS),
        _pad_head_gain(q_norm_fox, FOX_HEADS), _pad_head_gain(k_norm_fox, FOX_HEADS),
        bf_pad, positions)

    out_a = _swa_call(qa, ka, va, sinks)
    out_b = _fox_call(qf, kf, vf)

    x1, h2 = _post_call(x, mod3, g1, wg, out_a, out_b, w_out_swa.astype(BF16),
                        w_out_fox.astype(BF16), w_o.astype(BF16), g2)

    t_all = bsz * seq
    h2f = h2.reshape(t_all, D_MODEL)
    keys = peer_sub_keys.reshape(PEER_HEADS * 2, PEER_NKEYS, PEER_HALF).astype(BF16)
    n_t, c0_t, r1_t, e1_t = _route_call(h2f, peer_w_query.T.astype(BF16), keys)
    vt_tiles = peer_v.astype(BF16).reshape(PEER_EXPERTS // TE_PEER, TE_PEER, D_MODEL)
    vt_tiles = vt_tiles.transpose(0, 2, 1)
    out = _peer_call(h2f, peer_u.astype(BF16), vt_tiles, n_t, c0_t, r1_t, e1_t,
                     x1.reshape(t_all, D_MODEL), mod3, seq // TT_PEER)
    return out.reshape(bsz, seq, D_MODEL)


def kernel(x, c, positions, w_mod, b_mod, norm1_g, w_in, q_norm_swa, k_norm_swa, sinks, q_norm_fox, k_norm_fox, b_forget, w_out_swa, w_out_fox, w_o, norm2_g, peer_w_query, peer_sub_keys, peer_u, peer_v):
    for l in range(w_mod.shape[0]):
        x = _layer(x, c, positions, w_mod[l], b_mod[l], norm1_g[l], w_in[l], q_norm_swa[l],
                   k_norm_swa[l], sinks[l], q_norm_fox[l], k_norm_fox[l], b_forget[l],
                   w_out_swa[l], w_out_fox[l], w_o[l], norm2_g[l], peer_w_query[l],
                   peer_sub_keys[l], peer_u[l], peer_v[l])
    return x
```
